```python
import math
import jax
import jax.numpy as jnp
from jax import lax
import numpy as np

D_MODEL = 2048
BATCH = 4
SEQ = 4096
DEPTH = 4

HEAD_DIM = 128
N_HEADS_A = 8
N_KV_A = 2
N_HEADS_B = 8
N_KV_B = 2
Q_A = N_HEADS_A * HEAD_DIM
KV_A = N_KV_A * HEAD_DIM
Q_B = N_HEADS_B * HEAD_DIM
KV_B = N_KV_B * HEAD_DIM
D_MIX = Q_A + Q_B
D_IN = Q_A + 2 * KV_A + Q_B + 2 * KV_B
SPLITS = (Q_A, Q_A + KV_A, Q_A + 2 * KV_A, Q_A + 2 * KV_A + Q_B, Q_A + 2 * KV_A + Q_B + KV_B)
WINDOW = 128
BLOCK = 128
GRID_W = 64
ROPE_THETA = 10000.0
ROPE_AXIS_DIM = HEAD_DIM // 2
N_BUCKETS = 32
MAX_DISTANCE = 128
D_FF = 7168
N_EXPERTS = 8
TOP_K = 2
N_DENSE = (DEPTH + 1) // 2
N_MOE = DEPTH // 2
EPS = 1e-6
NEG_INF = -1e30

kernel_name = 'hybrid_headgroup_encoder'


def rmsnorm(x, g):
    xf = x.astype(jnp.float32)
    y = xf * lax.rsqrt(jnp.mean(xf * xf, axis=-1, keepdims=True) + EPS)
    return (y * g.astype(jnp.float32)).astype(x.dtype)


def t5_bucket(rel):
    half = N_BUCKETS // 2
    max_exact = half // 2
    side = jnp.where(rel > 0, half, 0)
    n = jnp.abs(rel)
    nf = jnp.maximum(n, 1).astype(jnp.float32)
    large = max_exact + (jnp.log(nf / max_exact) / math.log(MAX_DISTANCE / max_exact)
                         * (half - max_exact)).astype(jnp.int32)
    large = jnp.minimum(large, half - 1)
    return side + jnp.where(n < max_exact, n, large)


def window_attention(q, k, v, sink, rel_table):
    B, S, _, _ = q.shape
    nb = S // BLOCK
    G = N_HEADS_A // N_KV_A
    qb = q.reshape(B, nb, BLOCK, N_KV_A, G, HEAD_DIM)
    pad = ((0, 0), (BLOCK, BLOCK), (0, 0), (0, 0))
    kp = jnp.pad(k, pad).reshape(B, nb + 2, BLOCK, N_KV_A, HEAD_DIM)
    vp = jnp.pad(v, pad).reshape(B, nb + 2, BLOCK, N_KV_A, HEAD_DIM)
    kb = jnp.concatenate([kp[:, :-2], kp[:, 1:-1], kp[:, 2:]], axis=2)
    vb = jnp.concatenate([vp[:, :-2], vp[:, 1:-1], vp[:, 2:]], axis=2)
    logits = jnp.einsum('bnqkgd,bnskd->bnkgqs', qb, kb,
                        preferred_element_type=jnp.float32) * (HEAD_DIM ** -0.5)
    qi = jnp.arange(BLOCK, dtype=jnp.int32)[:, None]
    si = jnp.arange(3 * BLOCK, dtype=jnp.int32)[None, :]
    rel = si - BLOCK - qi
    bias = rel_table[t5_bucket(rel)].astype(jnp.float32)
    bias = bias.transpose(2, 0, 1).reshape(N_KV_A, G, BLOCK, 3 * BLOCK)
    kpos = (jnp.arange(nb, dtype=jnp.int32)[:, None, None] - 1) * BLOCK + si[None]
    valid = (jnp.abs(rel)[None] <= WINDOW) & (kpos >= 0) & (kpos < S)
    logits = jnp.where(valid[None, :, None, None], logits + bias[None, None], NEG_INF)
    sink_l = sink.astype(jnp.float32).reshape(1, 1, N_KV_A, G, 1, 1)
    m = jnp.maximum(jnp.max(logits, axis=-1, keepdims=True), sink_l)
    p = jnp.exp(logits - m)
    denom = jnp.sum(p, axis=-1, keepdims=True) + jnp.exp(sink_l - m)
    out = jnp.einsum('bnkgqs,bnskd->bnqkgd', (p / denom).astype(v.dtype), vb)
    return out.reshape(B, S, N_HEADS_A * HEAD_DIM)


def axial_rope_angles(rows):
    row = jnp.repeat(jnp.arange(rows, dtype=jnp.float32), GRID_W)
    col = jnp.tile(jnp.arange(GRID_W, dtype=jnp.float32), rows)
    inv = 1.0 / (ROPE_THETA ** (jnp.arange(0, ROPE_AXIS_DIM, 2, dtype=jnp.float32) / ROPE_AXIS_DIM))
    return row[:, None] * inv[None, :], col[:, None] * inv[None, :]


def rotate_half(x, ang):
    x1, x2 = jnp.split(x, 2, axis=-1)
    c = jnp.cos(ang)[None, :, None, :]
    s = jnp.sin(ang)[None, :, None, :]
    return jnp.concatenate([x1 * c - x2 * s, x2 * c + x1 * s], axis=-1)


def axial_rope(x, ang_row, ang_col):
    xf = x.astype(jnp.float32)
    y = jnp.concatenate([rotate_half(xf[..., :ROPE_AXIS_DIM], ang_row),
                         rotate_half(xf[..., ROPE_AXIS_DIM:], ang_col)], axis=-1)
    return y.astype(x.dtype)


def global_attention(q, k, v):
    B, S, _, _ = q.shape
    nb = S // BLOCK
    G = N_HEADS_B // N_KV_B
    qb = q.reshape(B, nb, BLOCK, N_KV_B, G, HEAD_DIM).transpose(1, 0, 2, 3, 4, 5)

    def one_block(q_blk):
        logits = jnp.einsum('bqkgd,bskd->bkgqs', q_blk, k,
                            preferred_element_type=jnp.float32) * (HEAD_DIM ** -0.5)
        p = jax.nn.softmax(logits, axis=-1)
        return jnp.einsum('bkgqs,bskd->bqkgd', p.astype(v.dtype), v)

    out = lax.map(one_block, qb)
    return out.transpose(1, 0, 2, 3, 4, 5).reshape(B, S, N_HEADS_B * HEAD_DIM)


def swiglu(h, wg, wu, wd):
    a = jnp.einsum('bsd,df->bsf', h, wg)
    b = jnp.einsum('bsd,df->bsf', h, wu)
    return jnp.einsum('bsf,fd->bsd', jax.nn.silu(a) * b, wd)


def moe_swiglu(h, w_router, wg, wu, wd):
    logits = jnp.einsum('bsd,de->bse', h, w_router, preferred_element_type=jnp.float32)
    top_v, top_i = lax.top_k(logits, TOP_K)
    gates = jax.nn.softmax(top_v, axis=-1)
    combine = jnp.sum(jax.nn.one_hot(top_i, N_EXPERTS, dtype=jnp.float32)
                      * gates[..., None], axis=-2)
    y = jnp.zeros(h.shape, jnp.float32)
    for e in range(N_EXPERTS):
        y = y + combine[..., e:e + 1] * swiglu(h, wg[e], wu[e], wd[e]).astype(jnp.float32)
    return y.astype(h.dtype)


def setup_inputs(seed: int = 0) -> dict:
    key = jax.random.key(seed)
    ks = jax.random.split(key, 17)
    f32 = jnp.float32

    def w(k, shape, fan_in):
        return jax.random.normal(k, shape, f32) * (fan_in ** -0.5)

    def gain(k, shape):
        return 1.0 + 0.02 * jax.random.normal(k, shape, f32)

    return {
        'x': jax.random.normal(ks[0], (BATCH, SEQ, D_MODEL), f32),
        'norm_mix': gain(ks[1], (DEPTH, D_MODEL)),
        'w_in': w(ks[2], (DEPTH, D_MODEL, D_IN), D_MODEL),
        'w_out': w(ks[3], (DEPTH, D_MIX, D_MODEL), D_MIX),
        'sink_a': 0.5 * jax.random.normal(ks[4], (DEPTH, N_HEADS_A), f32),
        'rel_bias': 0.5 * jax.random.normal(ks[5], (N_BUCKETS, N_HEADS_A), f32),
        'q_norm_b': gain(ks[6], (DEPTH, HEAD_DIM)),
        'k_norm_b': gain(ks[7], (DEPTH, HEAD_DIM)),
        'norm_ffn': gain(ks[8], (DEPTH, D_MODEL)),
        'w_dense_gate': w(ks[9], (N_DENSE, D_MODEL, D_FF), D_MODEL),
        'w_dense_up': w(ks[10], (N_DENSE, D_MODEL, D_FF), D_MODEL),
        'w_dense_down': w(ks[11], (N_DENSE, D_FF, D_MODEL), D_FF),
        'w_router': w(ks[12], (N_MOE, D_MODEL, N_EXPERTS), D_MODEL),
        'w_exp_gate': w(ks[13], (N_MOE, N_EXPERTS, D_MODEL, D_FF), D_MODEL),
        'w_exp_up': w(ks[14], (N_MOE, N_EXPERTS, D_MODEL, D_FF), D_MODEL),
        'w_exp_down': w(ks[15], (N_MOE, N_EXPERTS, D_FF, D_MODEL), D_FF),
        'norm_final': gain(ks[16], (D_MODEL,)),
    }


def reference(x, norm_mix, w_in, w_out, sink_a, rel_bias, q_norm_b, k_norm_b, norm_ffn,
              w_dense_gate, w_dense_up, w_dense_down, w_router, w_exp_gate, w_exp_up,
              w_exp_down, norm_final):
    B, S, _ = x.shape
    rows = S // GRID_W
    ang_row, ang_col = axial_rope_angles(rows)
    for l in range(DEPTH):
        h = rmsnorm(x, norm_mix[l])
        proj = jnp.einsum('bsd,de->bse', h, w_in[l])
        qa, ka, va, qb, kb, vb = jnp.split(proj, SPLITS, axis=-1)
        qa = qa.reshape(B, S, N_HEADS_A, HEAD_DIM)
        ka = ka.reshape(B, S, N_KV_A, HEAD_DIM)
        va = va.reshape(B, S, N_KV_A, HEAD_DIM)
        out_a = window_attention(qa, ka, va, sink_a[l], rel_bias)
        qb = axial_rope(rmsnorm(qb.reshape(B, S, N_HEADS_B, HEAD_DIM), q_norm_b[l]), ang_row, ang_col)
        kb = axial_rope(rmsnorm(kb.reshape(B, S, N_KV_B, HEAD_DIM), k_norm_b[l]), ang_row, ang_col)
        vb = vb.reshape(B, S, N_KV_B, HEAD_DIM)
        out_b = global_attention(qb, kb, vb)
        mixed = jnp.concatenate([out_a, out_b], axis=-1)
        x = x + jnp.einsum('bse,ed->bsd', mixed, w_out[l])
        h = rmsnorm(x, norm_ffn[l])
        if l % 2 == 0:
            i = l // 2
            x = x + swiglu(h, w_dense_gate[i], w_dense_up[i], w_dense_down[i])
        else:
            i = l // 2
            x = x + moe_swiglu(h, w_router[i], w_exp_gate[i], w_exp_up[i], w_exp_down[i])
    return rmsnorm(x, norm_final)
```

```python
import functools
import math

import jax
import jax.numpy as jnp
from jax import lax
from jax.experimental import pallas as pl
from jax.experimental.pallas import tpu as pltpu

HEAD_DIM = 128
N_HEADS_A = 8
N_KV_A = 2
N_HEADS_B = 8
N_KV_B = 2
GROUP_A = N_HEADS_A // N_KV_A
GROUP_B = N_HEADS_B // N_KV_B
Q_A = N_HEADS_A * HEAD_DIM
KV_A = N_KV_A * HEAD_DIM
Q_B = N_HEADS_B * HEAD_DIM
KV_B = N_KV_B * HEAD_DIM
HALF_IN = Q_A + 2 * KV_A
WINDOW = 128
BLOCK = 128
GRID_W = 64
ROPE_THETA = 10000.0
ROPE_AXIS_DIM = HEAD_DIM // 2
N_BUCKETS = 32
MAX_DISTANCE = 128
TOP_K = 2
EPS = 1e-6
NEG_INF = -1e30
LANES = 128
QK_SCALE = HEAD_DIM ** -0.5
VMEM_LIMIT = 56 * 1024 * 1024


def _params(*sem):
    return pltpu.CompilerParams(dimension_semantics=sem, vmem_limit_bytes=VMEM_LIMIT)


def _rms(x, g):
    return x * lax.rsqrt(jnp.mean(x * x, axis=-1, keepdims=True) + EPS) * g


def _dot(a, b):
    return jnp.dot(a, b, preferred_element_type=jnp.float32)


def _dot_t(a, b):
    return lax.dot_general(a, b, (((1,), (1,)), ((), ())), preferred_element_type=jnp.float32)


def _rope(x, cos, sin_signed):
    lane = lax.broadcasted_iota(jnp.int32, x.shape, 1)
    swapped = jnp.where((lane & 32) == 0,
                        pltpu.roll(x, LANES - 32, axis=1), pltpu.roll(x, 32, axis=1))
    return x * cos + swapped * sin_signed


def _proj_kernel(x_ref, g_ref, w_ref, cos_ref, sin_ref, qg_ref, kg_ref, o_ref):
    h = _rms(x_ref[...], g_ref[...]).astype(jnp.bfloat16)
    y = _dot(h, w_ref[...])
    group = pl.program_id(0)

    @pl.when(group == 0)
    def _():
        o_ref[:, :Q_A] = (y[:, :Q_A] * QK_SCALE).astype(o_ref.dtype)
        o_ref[:, Q_A:] = y[:, Q_A:].astype(o_ref.dtype)

    @pl.when(group == 1)
    def _():
        cos = cos_ref[...]
        sin = sin_ref[...]
        for hh in range(N_HEADS_B + N_KV_B):
            sl = slice(hh * HEAD_DIM, (hh + 1) * HEAD_DIM)
            if hh < N_HEADS_B:
                v = _rope(_rms(y[:, sl], qg_ref[...]), cos, sin) * QK_SCALE
            else:
                v = _rope(_rms(y[:, sl], kg_ref[...]), cos, sin)
            o_ref[:, sl] = v.astype(o_ref.dtype)
        o_ref[:, Q_B + KV_B:] = y[:, Q_B + KV_B:].astype(o_ref.dtype)


def _proj(x2, gain, w_bf16, cos, sin, qg, kg, seq):
    t, d = x2.shape
    tm = min(512, seq)
    spt = seq // tm
    return pl.pallas_call(
        _proj_kernel,
        grid=(2, t // tm),
        in_specs=[
            pl.BlockSpec((tm, d), lambda j, i: (i, 0)),
            pl.BlockSpec((1, d), lambda j, i: (0, 0)),
            pl.BlockSpec((d, HALF_IN), lambda j, i: (0, j)),
            pl.BlockSpec((tm, HEAD_DIM), lambda j, i: (i % spt, 0)),
            pl.BlockSpec((tm, HEAD_DIM), lambda j, i: (i % spt, 0)),
            pl.BlockSpec((1, HEAD_DIM), lambda j, i: (0, 0)),
            pl.BlockSpec((1, HEAD_DIM), lambda j, i: (0, 0)),
        ],
        out_specs=pl.BlockSpec((tm, HALF_IN), lambda j, i: (i, j)),
        out_shape=jax.ShapeDtypeStruct((t, 2 * HALF_IN), jnp.bfloat16),
        compiler_params=_params("arbitrary", "arbitrary"),
        name="proj",
    )(x2, gain, w_bf16, cos, sin, qg, kg)


def _win_kernel(sink_ref, q_ref, k_ref, v_ref, bias_ref, o_ref, *, qb, nb, seq):
    step = pl.program_id(1)
    for jb in range(qb):
        n = step * qb + jb
        ws = pl.multiple_of(jnp.clip((n - 1) * BLOCK, 0, seq - 3 * BLOCK), BLOCK)
        variant = jnp.where(n == 0, 0, jnp.where(n == nb - 1, 2, 1))
        rows = slice(jb * BLOCK, (jb + 1) * BLOCK)
        for kvh in range(N_KV_A):
            cols = slice(kvh * HEAD_DIM, (kvh + 1) * HEAD_DIM)
            kw = k_ref[pl.ds(ws, 3 * BLOCK), cols]
            vw = v_ref[pl.ds(ws, 3 * BLOCK), cols]
            for g in range(GROUP_A):
                hd = kvh * GROUP_A + g
                hcols = slice(hd * HEAD_DIM, (hd + 1) * HEAD_DIM)
                s = _dot_t(q_ref[rows, hcols], kw) + bias_ref[variant, hd]
                sink = sink_ref[hd]
                m = jnp.maximum(jnp.max(s, axis=-1, keepdims=True), sink)
                p = jnp.exp(s - m)
                denom = jnp.sum(p, axis=-1, keepdims=True) + jnp.exp(sink - m)
                o = _dot(p.astype(jnp.bfloat16), vw) / denom
                o_ref[rows, hcols] = o.astype(o_ref.dtype)


def _window_attention(proj, sink, bias, batch, seq):
    nb = seq // BLOCK
    qb = min(4, nb)
    spb = seq // (qb * BLOCK)
    kern = functools.partial(_win_kernel, qb=qb, nb=nb, seq=seq)
    return pl.pallas_call(
        kern,
        grid_spec=pltpu.PrefetchScalarGridSpec(
            num_scalar_prefetch=1,
            grid=(batch, spb),
            in_specs=[
                pl.BlockSpec((qb * BLOCK, Q_A), lambda b, s, sk: (b * spb + s, 0)),
                pl.BlockSpec((seq, KV_A), lambda b, s, sk: (b, Q_A // KV_A)),
                pl.BlockSpec((seq, KV_A), lambda b, s, sk: (b, Q_A // KV_A + 1)),
                pl.BlockSpec(bias.shape, lambda b, s, sk: (0, 0, 0, 0)),
            ],
            out_specs=pl.BlockSpec((qb * BLOCK, Q_A), lambda b, s, sk: (b * spb + s, 0)),
        ),
        out_shape=jax.ShapeDtypeStruct((batch * seq, Q_A), jnp.bfloat16),
        compiler_params=_params("arbitrary", "arbitrary"),
        name="window_attention",
    )(sink, proj, proj, proj, bias)


def _glob_kernel(q_ref, k_ref, v_ref, o_ref):
    k = k_ref[...]
    v = v_ref[...]
    for g in range(GROUP_B):
        cols = slice(g * HEAD_DIM, (g + 1) * HEAD_DIM)
        s = _dot_t(q_ref[:, cols], k)
        m = jnp.max(s, axis=-1, keepdims=True)
        p = jnp.exp(s - m)
        denom = jnp.sum(p, axis=-1, keepdims=True)
        o = _dot(p.astype(jnp.bfloat16), v) / denom
        o_ref[:, cols] = o.astype(o_ref.dtype)


def _global_attention(proj, batch, seq):
    tq = min(256, seq)
    spb = seq // tq
    gw = GROUP_B * HEAD_DIM
    q_blk0 = HALF_IN // gw
    k_blk0 = (HALF_IN + Q_B) // HEAD_DIM
    v_blk0 = (HALF_IN + Q_B + KV_B) // HEAD_DIM
    return pl.pallas_call(
        _glob_kernel,
        grid=(batch, N_KV_B, spb),
        in_specs=[
            pl.BlockSpec((tq, gw), lambda b, h, s: (b * spb + s, q_blk0 + h)),
            pl.BlockSpec((seq, HEAD_DIM), lambda b, h, s: (b, k_blk0 + h)),
            pl.BlockSpec((seq, HEAD_DIM), lambda b, h, s: (b, v_blk0 + h)),
        ],
        out_specs=pl.BlockSpec((tq, gw), lambda b, h, s: (b * spb + s, h)),
        out_shape=jax.ShapeDtypeStruct((batch * seq, Q_B), jnp.bfloat16),
        compiler_params=_params("arbitrary", "arbitrary", "arbitrary"),
        name="global_attention",
    )(proj, proj, proj)


def _outproj_kernel(a_ref, b_ref, wa_ref, wb_ref, x_ref, o_ref):
    o_ref[...] = x_ref[...] + _dot(a_ref[...], wa_ref[...]) + _dot(b_ref[...], wb_ref[...])


def _outproj(out_a, out_b, w_bf16, x2):
    t, d = x2.shape
    tm = min(1024, t)
    tn = min(1024, d)
    assert Q_A == Q_B
    return pl.pallas_call(
        _outproj_kernel,
        grid=(t // tm, d // tn),
        in_specs=[
            pl.BlockSpec((tm, Q_A), lambda i, j: (i, 0)),
            pl.BlockSpec((tm, Q_B), lambda i, j: (i, 0)),
            pl.BlockSpec((Q_A, tn), lambda i, j: (0, j)),
            pl.BlockSpec((Q_B, tn), lambda i, j: (1, j)),
            pl.BlockSpec((tm, tn), lambda i, j: (i, j)),
        ],
        out_specs=pl.BlockSpec((tm, tn), lambda i, j: (i, j)),
        out_shape=jax.ShapeDtypeStruct((t, d), jnp.float32),
        compiler_params=_params("arbitrary", "arbitrary"),
        name="outproj",
    )(out_a, out_b, w_bf16, w_bf16, x2)


def _swiglu_chunk(h, wg, wu, wd):
    a = _dot(h, wg)
    b = _dot(h, wu)
    act = (a * jax.nn.sigmoid(a) * b).astype(jnp.bfloat16)
    return _dot(act, wd)


def _ffn_kernel(x_ref, g_ref, wg_ref, wu_ref, wd_ref, o_ref, h_ref):
    k = pl.program_id(1)

    @pl.when(k == 0)
    def _():
        x = x_ref[...]
        h_ref[...] = _rms(x, g_ref[...]).astype(h_ref.dtype)
        o_ref[...] = x

    o_ref[...] += _swiglu_chunk(h_ref[...], wg_ref[...], wu_ref[...], wd_ref[...])


def _ffn_dense(x2, gain, wg, wu, wd):
    t, d = x2.shape
    f = wg.shape[1]
    tm = min(512, t)
    tf = min(512, f)
    return pl.pallas_call(
        _ffn_kernel,
        grid=(t // tm, f // tf),
        in_specs=[
            pl.BlockSpec((tm, d), lambda i, k: (i, 0)),
            pl.BlockSpec((1, d), lambda i, k: (0, 0)),
            pl.BlockSpec((d, tf), lambda i, k: (0, k)),
            pl.BlockSpec((d, tf), lambda i, k: (0, k)),
            pl.BlockSpec((tf, d), lambda i, k: (k, 0)),
        ],
        out_specs=pl.BlockSpec((tm, d), lambda i, k: (i, 0)),
        out_shape=jax.ShapeDtypeStruct((t, d), jnp.float32),
        scratch_shapes=[pltpu.VMEM((tm, d), jnp.bfloat16)],
        compiler_params=_params("arbitrary", "arbitrary"),
        name="ffn_dense",
    )(x2, gain, wg, wu, wd)


def _router_kernel(x_ref, g_ref, w_ref, o_ref, *, n_experts):
    h = _rms(x_ref[...], g_ref[...])
    logits = jnp.dot(h, w_ref[...], preferred_element_type=jnp.float32,
                     precision=lax.Precision.HIGHEST)
    lane = lax.broadcasted_iota(jnp.int32, logits.shape, 1)
    logits = jnp.where(lane < n_experts, logits, -jnp.inf)
    v1 = jnp.max(logits, axis=-1, keepdims=True)
    i1 = jnp.min(jnp.where(logits == v1, lane, LANES), axis=-1, keepdims=True)
    rest = jnp.where(lane == i1, -jnp.inf, logits)
    v2 = jnp.max(rest, axis=-1, keepdims=True)
    i2 = jnp.min(jnp.where(rest == v2, lane, LANES), axis=-1, keepdims=True)
    e2 = jnp.exp(v2 - v1)
    g1 = 1.0 / (1.0 + e2)
    g2 = e2 / (1.0 + e2)
    o_ref[...] = jnp.where(lane == 0, i1.astype(jnp.float32),
                           jnp.where(lane == 1, i2.astype(jnp.float32),
                                     jnp.where(lane == 2, g1, jnp.where(lane == 3, g2, 0.0))))


def _router(x2, gain, w_pad, n_experts):
    t, d = x2.shape
    tm = min(512, t)
    return pl.pallas_call(
        functools.partial(_router_kernel, n_experts=n_experts),
        grid=(t // tm,),
        in_specs=[
            pl.BlockSpec((tm, d), lambda i: (i, 0)),
            pl.BlockSpec((1, d), lambda i: (0, 0)),
            pl.BlockSpec((d, LANES), lambda i: (0, 0)),
        ],
        out_specs=pl.BlockSpec((tm, LANES), lambda i: (i, 0)),
        out_shape=jax.ShapeDtypeStruct((t, LANES), jnp.float32),
        compiler_params=_params("arbitrary"),
        name="router",
    )(x2, gain, w_pad)


def _moe_kernel(te_ref, nu_ref, src_ref, x_hbm, g_ref, wg_ref, wu_ref, wd_ref, o_hbm,
                acc_ref, h_ref, sem_in, sem_out, *, tm):
    i = pl.program_id(0)
    k = pl.program_id(1)
    nk = pl.num_programs(1)
    used = i < nu_ref[0]

    def row_copy(r):
        return pltpu.make_async_copy(x_hbm.at[pl.ds(src_ref[i * tm + r], 1)],
                                     acc_ref.at[pl.ds(r, 1)], sem_in)

    def out_copy():
        return pltpu.make_async_copy(acc_ref, o_hbm.at[pl.ds(pl.multiple_of(i * tm, tm), tm)],
                                     sem_out)

    @pl.when(jnp.logical_and(used, k == 0))
    def _():
        def start(r, c):
            row_copy(r).start()
            return c
        lax.fori_loop(0, tm, start, 0)

        def wait(r, c):
            row_copy(r).wait()
            return c
        lax.fori_loop(0, tm, wait, 0)
        h_ref[...] = _rms(acc_ref[...], g_ref[...]).astype(h_ref.dtype)
        acc_ref[...] = jnp.zeros_like(acc_ref)

    @pl.when(jnp.logical_and(jnp.logical_not(used), k == 0))
    def _():
        acc_ref[...] = jnp.zeros_like(acc_ref)

    @pl.when(used)
    def _():
        wg = wg_ref[0].astype(jnp.bfloat16)
        wu = wu_ref[0].astype(jnp.bfloat16)
        wd = wd_ref[0].astype(jnp.bfloat16)
        acc_ref[...] += _swiglu_chunk(h_ref[...], wg, wu, wd)

    @pl.when(k == nk - 1)
    def _():
        out_copy().start()
        out_copy().wait()


def _moe_ffn(x2, gain, wg, wu, wd, tile_expert, n_used, row_src, tm, n_tiles):
    t, d = x2.shape
    f = wg.shape[2]
    tf = min(256, f)
    nk = f // tf

    def wmap_col(i, k, te, nu, src):
        return (te[i], 0, jnp.where(i < nu[0], k, nk - 1))

    def wmap_row(i, k, te, nu, src):
        return (te[i], jnp.where(i < nu[0], k, nk - 1), 0)

    return pl.pallas_call(
        functools.partial(_moe_kernel, tm=tm),
        grid_spec=pltpu.PrefetchScalarGridSpec(
            num_scalar_prefetch=3,
            grid=(n_tiles, nk),
            in_specs=[
                pl.BlockSpec(memory_space=pl.ANY),
                pl.BlockSpec((1, d), lambda i, k, te, nu, src: (0, 0)),
                pl.BlockSpec((1, d, tf), wmap_col),
                pl.BlockSpec((1, d, tf), wmap_col),
                pl.BlockSpec((1, tf, d), wmap_row),
            ],
            out_specs=pl.BlockSpec(memory_space=pl.ANY),
            scratch_shapes=[
                pltpu.VMEM((tm, d), jnp.float32),
                pltpu.VMEM((tm, d), jnp.bfloat16),
                pltpu.SemaphoreType.DMA(()),
                pltpu.SemaphoreType.DMA(()),
            ],
        ),
        out_shape=jax.ShapeDtypeStruct((n_tiles * tm, d), jnp.float32),
        compiler_params=_params("arbitrary", "arbitrary"),
        name="moe_ffn",
    )(tile_expert, n_used, row_src, x2, gain, wg, wu, wd)


def _combine_kernel(pos_ref, x_ref, gate_ref, fg_ref, y_hbm, o_ref, buf_ref, sem, *, tc, t, final):
    i = pl.program_id(0)

    def row_copy(s, r):
        return pltpu.make_async_copy(y_hbm.at[pl.ds(pos_ref[s * t + i * tc + r], 1)],
                                     buf_ref.at[s, pl.ds(r, 1)], sem)

    def start(r, c):
        row_copy(0, r).start()
        row_copy(1, r).start()
        return c
    lax.fori_loop(0, tc, start, 0)

    def wait(r, c):
        row_copy(0, r).wait()
        row_copy(1, r).wait()
        return c
    lax.fori_loop(0, tc, wait, 0)
    gate = gate_ref[...]
    y = x_ref[...] + gate[:, 2:3] * buf_ref[0] + gate[:, 3:4] * buf_ref[1]
    if final:
        y = _rms(y, fg_ref[...])
    o_ref[...] = y


def _combine(x2, routing, pos, y_sorted, final_gain, final):
    t, d = x2.shape
    tc = min(256, t)
    return pl.pallas_call(
        functools.partial(_combine_kernel, tc=tc, t=t, final=final),
        grid_spec=pltpu.PrefetchScalarGridSpec(
            num_scalar_prefetch=1,
            grid=(t // tc,),
            in_specs=[
                pl.BlockSpec((tc, d), lambda i, p: (i, 0)),
                pl.BlockSpec((tc, LANES), lambda i, p: (i, 0)),
                pl.BlockSpec((1, d), lambda i, p: (0, 0)),
                pl.BlockSpec(memory_space=pl.ANY),
            ],
            out_specs=pl.BlockSpec((tc, d), lambda i, p: (i, 0)),
            scratch_shapes=[
                pltpu.VMEM((TOP_K, tc, d), jnp.float32),
                pltpu.SemaphoreType.DMA(()),
            ],
        ),
        out_shape=jax.ShapeDtypeStruct((t, d), jnp.float32),
        compiler_params=_params("arbitrary"),
        name="combine",
    )(pos, x2, routing, final_gain, y_sorted)


def _final_norm_kernel(x_ref, g_ref, o_ref):
    o_ref[...] = _rms(x_ref[...], g_ref[...])


def _final_norm(x2, gain):
    t, d = x2.shape
    tm = min(1024, t)
    return pl.pallas_call(
        _final_norm_kernel,
        grid=(t // tm,),
        in_specs=[pl.BlockSpec((tm, d), lambda i: (i, 0)), pl.BlockSpec((1, d), lambda i: (0, 0))],
        out_specs=pl.BlockSpec((tm, d), lambda i: (i, 0)),
        out_shape=jax.ShapeDtypeStruct((t, d), jnp.float32),
        compiler_params=_params("arbitrary"),
        name="final_norm",
    )(x2, gain)


def _t5_bucket(rel):
    half = N_BUCKETS // 2
    max_exact = half // 2
    side = jnp.where(rel > 0, half, 0)
    n = jnp.abs(rel)
    nf = jnp.maximum(n, 1).astype(jnp.float32)
    large = max_exact + (jnp.log(nf / max_exact) / math.log(MAX_DISTANCE / max_exact)
                         * (half - max_exact)).astype(jnp.int32)
    large = jnp.minimum(large, half - 1)
    return side + jnp.where(n < max_exact, n, large)


def _window_bias(rel_table):
    qi = jnp.arange(BLOCK, dtype=jnp.int32)[None, :, None]
    si = jnp.arange(3 * BLOCK, dtype=jnp.int32)[None, None, :]
    off = (jnp.arange(3, dtype=jnp.int32) * BLOCK)[:, None, None]
    rel = si - off - qi
    bias = rel_table[_t5_bucket(rel)].astype(jnp.float32)
    bias = jnp.where((jnp.abs(rel) <= WINDOW)[..., None], bias, NEG_INF)
    return bias.transpose(0, 3, 1, 2)


def _rope_tables(seq):
    pos = jnp.arange(seq, dtype=jnp.int32)
    row = (pos // GRID_W).astype(jnp.float32)
    col = (pos % GRID_W).astype(jnp.float32)
    inv = 1.0 / (ROPE_THETA ** (jnp.arange(0, ROPE_AXIS_DIM, 2, dtype=jnp.float32) / ROPE_AXIS_DIM))
    ar = row[:, None] * inv[None, :]
    ac = col[:, None] * inv[None, :]
    cos = jnp.concatenate([jnp.cos(ar), jnp.cos(ar), jnp.cos(ac), jnp.cos(ac)], axis=-1)
    sin = jnp.concatenate([-jnp.sin(ar), jnp.sin(ar), -jnp.sin(ac), jnp.sin(ac)], axis=-1)
    return cos, sin


def _routing_plan(routing, n_experts, tm, n_tiles):
    t = routing.shape[0]
    experts = jnp.concatenate([routing[:, 0], routing[:, 1]]).astype(jnp.int32)
    onehot = (experts[:, None] == jnp.arange(n_experts, dtype=jnp.int32)[None, :]).astype(jnp.int32)
    csum = jnp.cumsum(onehot, axis=0)
    rank = jnp.sum(onehot * csum, axis=1) - 1
    counts = csum[-1]
    tiles_per = (counts + tm - 1) // tm
    tile_end = jnp.cumsum(tiles_per)
    starts = (tile_end - tiles_per) * tm
    pos = (jnp.sum(onehot * starts[None, :], axis=1) + rank).astype(jnp.int32)
    tok = jnp.arange(TOP_K * t, dtype=jnp.int32) % t
    row_src = jnp.zeros((n_tiles * tm,), jnp.int32).at[pos].set(tok)
    n_used = tile_end[-1].astype(jnp.int32)
    tile_id = jnp.minimum(jnp.arange(n_tiles, dtype=jnp.int32), n_used - 1)
    tile_expert = jnp.sum((tile_id[:, None] >= tile_end[None, :]).astype(jnp.int32), axis=1)
    return tile_expert.astype(jnp.int32), n_used.reshape(1), row_src, pos


def kernel(x, norm_mix, w_in, w_out, sink_a, rel_bias, q_norm_b, k_norm_b, norm_ffn,
           w_dense_gate, w_dense_up, w_dense_down, w_router, w_exp_gate, w_exp_up,
           w_exp_down, norm_final):
    batch, seq, d = x.shape
    depth = w_in.shape[0]
    n_experts = w_router.shape[-1]
    t = batch * seq
    bf16 = jnp.bfloat16
    x2 = x.reshape(t, d)
    cos, sin = _rope_tables(seq)
    bias = _window_bias(rel_bias)
    moe_tm = min(1024, t)
    moe_tiles = (TOP_K * t) // moe_tm + n_experts
    for l in range(depth):
        proj = _proj(x2, norm_mix[l][None], w_in[l].astype(bf16), cos, sin,
                     q_norm_b[l][None], k_norm_b[l][None], seq)
        out_a = _window_attention(proj, sink_a[l], bias, batch, seq)
        out_b = _global_attention(proj, batch, seq)
        x2 = _outproj(out_a, out_b, w_out[l].astype(bf16), x2)
        last = l == depth - 1
        i = l // 2
        if l % 2 == 0:
            x2 = _ffn_dense(x2, norm_ffn[l][None], w_dense_gate[i].astype(bf16),
                            w_dense_up[i].astype(bf16), w_dense_down[i].astype(bf16))
            if last:
                x2 = _final_norm(x2, norm_final[None])
        else:
            w_pad = jnp.pad(w_router[i], ((0, 0), (0, LANES - n_experts)))
            routing = _router(x2, norm_ffn[l][None], w_pad, n_experts)
            tile_expert, n_used, row_src, pos = _routing_plan(routing, n_experts, moe_tm, moe_tiles)
            y_sorted = _moe_ffn(x2, norm_ffn[l][None], w_exp_gate[i], w_exp_up[i], w_exp_down[i],
                                tile_expert, n_used, row_src, moe_tm, moe_tiles)
            x2 = _combine(x2, routing, pos, y_sorted, norm_final[None], last)
    return x2.reshape(batch, seq, d)
```

```python
import functools
import math

import jax
import jax.numpy as jnp
from jax import lax
from jax.experimental import pallas as pl
from jax.experimental.pallas import tpu as pltpu

HEAD_DIM = 128
N_HEADS_A = 8
N_KV_A = 2
N_HEADS_B = 8
N_KV_B = 2
GROUP_A = N_HEADS_A // N_KV_A
GROUP_B = N_HEADS_B // N_KV_B
Q_A = N_HEADS_A * HEAD_DIM
KV_A = N_KV_A * HEAD_DIM
Q_B = N_HEADS_B * HEAD_DIM
KV_B = N_KV_B * HEAD_DIM
HALF_IN = Q_A + 2 * KV_A
WINDOW = 128
BLOCK = 128
GRID_W = 64
ROPE_THETA = 10000.0
ROPE_AXIS_DIM = HEAD_DIM // 2
N_BUCKETS = 32
MAX_DISTANCE = 128
TOP_K = 2
EPS = 1e-6
NEG_INF = -1e30
LANES = 128
QK_SCALE = HEAD_DIM ** -0.5
VMEM_LIMIT = 56 * 1024 * 1024
MOE_SUB = 256
DMA_UNROLL = 8


def _params(*sem):
    return pltpu.CompilerParams(dimension_semantics=sem, vmem_limit_bytes=VMEM_LIMIT)


def _rms(x, g):
    return x * lax.rsqrt(jnp.mean(x * x, axis=-1, keepdims=True) + EPS) * g


def _dot(a, b):
    return jnp.dot(a, b, preferred_element_type=jnp.float32)


def _dot_t(a, b):
    return lax.dot_general(a, b, (((1,), (1,)), ((), ())), preferred_element_type=jnp.float32)


def _rope(x, cos, sin_signed):
    lane = lax.broadcasted_iota(jnp.int32, x.shape, 1)
    swapped = jnp.where((lane & 32) == 0,
                        pltpu.roll(x, LANES - 32, axis=1), pltpu.roll(x, 32, axis=1))
    return x * cos + swapped * sin_signed


def _proj_kernel(x_ref, g_ref, w_ref, cos_ref, sin_ref, qg_ref, kg_ref, o_ref):
    h = _rms(x_ref[...], g_ref[...]).astype(jnp.bfloat16)
    y = _dot(h, w_ref[...])
    group = pl.program_id(0)

    @pl.when(group == 0)
    def _():
        o_ref[:, :Q_A] = (y[:, :Q_A] * QK_SCALE).astype(o_ref.dtype)
        o_ref[:, Q_A:] = y[:, Q_A:].astype(o_ref.dtype)

    @pl.when(group == 1)
    def _():
        cos = cos_ref[...]
        sin = sin_ref[...]
        for hh in range(N_HEADS_B + N_KV_B):
            sl = slice(hh * HEAD_DIM, (hh + 1) * HEAD_DIM)
            if hh < N_HEADS_B:
                v = _rope(_rms(y[:, sl], qg_ref[...]), cos, sin) * QK_SCALE
            else:
                v = _rope(_rms(y[:, sl], kg_ref[...]), cos, sin)
            o_ref[:, sl] = v.astype(o_ref.dtype)
        o_ref[:, Q_B + KV_B:] = y[:, Q_B + KV_B:].astype(o_ref.dtype)


def _proj(x2, gain, w_bf16, cos, sin, qg, kg, seq):
    t, d = x2.shape
    tm = min(512, seq)
    spt = seq // tm
    return pl.pallas_call(
        _proj_kernel,
        grid=(2, t // tm),
        in_specs=[
            pl.BlockSpec((tm, d), lambda j, i: (i, 0)),
            pl.BlockSpec((1, d), lambda j, i: (0, 0)),
            pl.BlockSpec((d, HALF_IN), lambda j, i: (0, j)),
            pl.BlockSpec((tm, HEAD_DIM), lambda j, i: (i % spt, 0)),
            pl.BlockSpec((tm, HEAD_DIM), lambda j, i: (i % spt, 0)),
            pl.BlockSpec((1, HEAD_DIM), lambda j, i: (0, 0)),
            pl.BlockSpec((1, HEAD_DIM), lambda j, i: (0, 0)),
        ],
        out_specs=pl.BlockSpec((tm, HALF_IN), lambda j, i: (i, j)),
        out_shape=jax.ShapeDtypeStruct((t, 2 * HALF_IN), jnp.bfloat16),
        compiler_params=_params("arbitrary", "arbitrary"),
        name="proj",
    )(x2, gain, w_bf16, cos, sin, qg, kg)


def _win_kernel(sink_ref, q_ref, k_ref, v_ref, bias_ref, o_ref, *, qb, nb, seq):
    step = pl.program_id(1)
    for jb in range(qb):
        n = step * qb + jb
        ws = pl.multiple_of(jnp.clip((n - 1) * BLOCK, 0, seq - 3 * BLOCK), BLOCK)
        variant = jnp.where(n == 0, 0, jnp.where(n == nb - 1, 2, 1))
        rows = slice(jb * BLOCK, (jb + 1) * BLOCK)
        for kvh in range(N_KV_A):
            cols = slice(kvh * HEAD_DIM, (kvh + 1) * HEAD_DIM)
            kw = k_ref[pl.ds(ws, 3 * BLOCK), cols]
            vw = v_ref[pl.ds(ws, 3 * BLOCK), cols]
            for g in range(GROUP_A):
                hd = kvh * GROUP_A + g
                hcols = slice(hd * HEAD_DIM, (hd + 1) * HEAD_DIM)
                s = _dot_t(q_ref[rows, hcols], kw) + bias_ref[variant, hd]
                sink = sink_ref[hd]
                m = jnp.maximum(jnp.max(s, axis=-1, keepdims=True), sink)
                p = jnp.exp(s - m)
                denom = jnp.sum(p, axis=-1, keepdims=True) + jnp.exp(sink - m)
                o = _dot(p.astype(jnp.bfloat16), vw) / denom
                o_ref[rows, hcols] = o.astype(o_ref.dtype)


def _window_attention(proj, sink, bias, batch, seq):
    nb = seq // BLOCK
    qb = min(4, nb)
    spb = seq // (qb * BLOCK)
    kern = functools.partial(_win_kernel, qb=qb, nb=nb, seq=seq)
    return pl.pallas_call(
        kern,
        grid_spec=pltpu.PrefetchScalarGridSpec(
            num_scalar_prefetch=1,
            grid=(batch, spb),
            in_specs=[
                pl.BlockSpec((qb * BLOCK, Q_A), lambda b, s, sk: (b * spb + s, 0)),
                pl.BlockSpec((seq, KV_A), lambda b, s, sk: (b, Q_A // KV_A)),
                pl.BlockSpec((seq, KV_A), lambda b, s, sk: (b, Q_A // KV_A + 1)),
                pl.BlockSpec(bias.shape, lambda b, s, sk: (0, 0, 0, 0)),
            ],
            out_specs=pl.BlockSpec((qb * BLOCK, Q_A), lambda b, s, sk: (b * spb + s, 0)),
        ),
        out_shape=jax.ShapeDtypeStruct((batch * seq, Q_A), jnp.bfloat16),
        compiler_params=_params("arbitrary", "arbitrary"),
        name="window_attention",
    )(sink, proj, proj, proj, bias)


def _glob_kernel(q_ref, k_ref, v_ref, o_ref):
    k = k_ref[...]
    v = v_ref[...]
    for g in range(GROUP_B):
        cols = slice(g * HEAD_DIM, (g + 1) * HEAD_DIM)
        s = _dot_t(q_ref[:, cols], k)
        m = jnp.max(s, axis=-1, keepdims=True)
        p = jnp.exp(s - m)
        denom = jnp.sum(p, axis=-1, keepdims=True)
        o = _dot(p.astype(jnp.bfloat16), v) / denom
        o_ref[:, cols] = o.astype(o_ref.dtype)


def _global_attention(proj, batch, seq):
    tq = min(256, seq)
    spb = seq // tq
    gw = GROUP_B * HEAD_DIM
    q_blk0 = HALF_IN // gw
    k_blk0 = (HALF_IN + Q_B) // HEAD_DIM
    v_blk0 = (HALF_IN + Q_B + KV_B) // HEAD_DIM
    return pl.pallas_call(
        _glob_kernel,
        grid=(batch, N_KV_B, spb),
        in_specs=[
            pl.BlockSpec((tq, gw), lambda b, h, s: (b * spb + s, q_blk0 + h)),
            pl.BlockSpec((seq, HEAD_DIM), lambda b, h, s: (b, k_blk0 + h)),
            pl.BlockSpec((seq, HEAD_DIM), lambda b, h, s: (b, v_blk0 + h)),
        ],
        out_specs=pl.BlockSpec((tq, gw), lambda b, h, s: (b * spb + s, h)),
        out_shape=jax.ShapeDtypeStruct((batch * seq, Q_B), jnp.bfloat16),
        compiler_params=_params("arbitrary", "arbitrary", "arbitrary"),
        name="global_attention",
    )(proj, proj, proj)


def _outproj_kernel(a_ref, b_ref, wa_ref, wb_ref, x_ref, o_ref):
    o_ref[...] = x_ref[...] + _dot(a_ref[...], wa_ref[...]) + _dot(b_ref[...], wb_ref[...])


def _outproj(out_a, out_b, w_bf16, x2):
    t, d = x2.shape
    tm = min(1024, t)
    tn = min(1024, d)
    assert Q_A == Q_B
    return pl.pallas_call(
        _outproj_kernel,
        grid=(t // tm, d // tn),
        in_specs=[
            pl.BlockSpec((tm, Q_A), lambda i, j: (i, 0)),
            pl.BlockSpec((tm, Q_B), lambda i, j: (i, 0)),
            pl.BlockSpec((Q_A, tn), lambda i, j: (0, j)),
            pl.BlockSpec((Q_B, tn), lambda i, j: (1, j)),
            pl.BlockSpec((tm, tn), lambda i, j: (i, j)),
        ],
        out_specs=pl.BlockSpec((tm, tn), lambda i, j: (i, j)),
        out_shape=jax.ShapeDtypeStruct((t, d), jnp.float32),
        compiler_params=_params("arbitrary", "arbitrary"),
        name="outproj",
    )(out_a, out_b, w_bf16, w_bf16, x2)


def _swiglu_chunk(h, wg, wu, wd):
    a = _dot(h, wg)
    b = _dot(h, wu)
    act = (a * jax.nn.sigmoid(a) * b).astype(jnp.bfloat16)
    return _dot(act, wd)


def _ffn_kernel(x_ref, g_ref, wg_ref, wu_ref, wd_ref, o_ref, h_ref):
    k = pl.program_id(1)

    @pl.when(k == 0)
    def _():
        x = x_ref[...]
        h_ref[...] = _rms(x, g_ref[...]).astype(h_ref.dtype)
        o_ref[...] = x

    o_ref[...] += _swiglu_chunk(h_ref[...], wg_ref[...], wu_ref[...], wd_ref[...])


def _ffn_dense(x2, gain, wg, wu, wd):
    t, d = x2.shape
    f = wg.shape[1]
    tm = min(512, t)
    tf = next(c for c in (512, 256, 128) if f % c == 0)
    return pl.pallas_call(
        _ffn_kernel,
        grid=(t // tm, f // tf),
        in_specs=[
            pl.BlockSpec((tm, d), lambda i, k: (i, 0)),
            pl.BlockSpec((1, d), lambda i, k: (0, 0)),
            pl.BlockSpec((d, tf), lambda i, k: (0, k)),
            pl.BlockSpec((d, tf), lambda i, k: (0, k)),
            pl.BlockSpec((tf, d), lambda i, k: (k, 0)),
        ],
        out_specs=pl.BlockSpec((tm, d), lambda i, k: (i, 0)),
        out_shape=jax.ShapeDtypeStruct((t, d), jnp.float32),
        scratch_shapes=[pltpu.VMEM((tm, d), jnp.bfloat16)],
        compiler_params=_params("arbitrary", "arbitrary"),
        name="ffn_dense",
    )(x2, gain, wg, wu, wd)


def _router_kernel(x_ref, g_ref, w_ref, o_ref, *, n_experts):
    h = _rms(x_ref[...], g_ref[...])
    logits = jnp.dot(h, w_ref[...], preferred_element_type=jnp.float32,
                     precision=lax.Precision.HIGHEST)
    lane = lax.broadcasted_iota(jnp.int32, logits.shape, 1)
    logits = jnp.where(lane < n_experts, logits, -jnp.inf)
    v1 = jnp.max(logits, axis=-1, keepdims=True)
    i1 = jnp.min(jnp.where(logits == v1, lane, LANES), axis=-1, keepdims=True)
    rest = jnp.where(lane == i1, -jnp.inf, logits)
    v2 = jnp.max(rest, axis=-1, keepdims=True)
    i2 = jnp.min(jnp.where(rest == v2, lane, LANES), axis=-1, keepdims=True)
    e2 = jnp.exp(v2 - v1)
    g1 = 1.0 / (1.0 + e2)
    g2 = e2 / (1.0 + e2)
    o_ref[...] = jnp.where(lane == 0, i1.astype(jnp.float32),
                           jnp.where(lane == 1, i2.astype(jnp.float32),
                                     jnp.where(lane == 2, g1, jnp.where(lane == 3, g2, 0.0))))


def _router(x2, gain, w_pad, n_experts):
    t, d = x2.shape
    tm = min(512, t)
    return pl.pallas_call(
        functools.partial(_router_kernel, n_experts=n_experts),
        grid=(t // tm,),
        in_specs=[
            pl.BlockSpec((tm, d), lambda i: (i, 0)),
            pl.BlockSpec((1, d), lambda i: (0, 0)),
            pl.BlockSpec((d, LANES), lambda i: (0, 0)),
        ],
        out_specs=pl.BlockSpec((tm, LANES), lambda i: (i, 0)),
        out_shape=jax.ShapeDtypeStruct((t, LANES), jnp.float32),
        compiler_params=_params("arbitrary"),
        name="router",
    )(x2, gain, w_pad)


def _moe_kernel(te_ref, nu_ref, nsub_ref, src_ref, x_hbm, g_ref, wg_ref, wu_ref, wd_ref, o_hbm,
                xg_ref, acc_ref, h_ref, sem_in, sem_out, *, tm, rows_per_step, n_tiles):
    i = pl.program_id(0)
    k = pl.program_id(1)
    nk = pl.num_programs(1)
    n_used = nu_ref[0]
    used = i < n_used
    slot = i % 2
    buf_rows = xg_ref.shape[1]

    def row_copy(tile, r, s):
        tok = src_ref[tile * tm + jnp.minimum(r, tm - 1)]
        return pltpu.make_async_copy(x_hbm.at[pl.ds(tok, 1)], xg_ref.at[s, pl.ds(r, 1)],
                                     sem_in.at[s])

    def wait_rows(s):
        pltpu.make_async_copy(xg_ref.at[s], xg_ref.at[s], sem_in.at[s]).wait()

    def out_copy(tile):
        return pltpu.make_async_copy(
            acc_ref, o_hbm.at[pl.ds(pl.multiple_of(tile * tm, tm), tm)], sem_out)

    @pl.when(jnp.logical_and(i == 0, k == 0))
    def _():
        def start(r, c):
            for u in range(DMA_UNROLL):
                row_copy(0, r * DMA_UNROLL + u, 0).start()
            return c
        lax.fori_loop(0, buf_rows // DMA_UNROLL, start, 0)

    @pl.when(k == 0)
    def _():
        @pl.when(i <= n_used)
        def _():
            wait_rows(slot)

        @pl.when(used)
        def _():
            h_ref[...] = _rms(xg_ref[slot, pl.ds(0, tm), :], g_ref[...]).astype(h_ref.dtype)

        @pl.when(i > 0)
        def _():
            out_copy(i - 1).wait()
        acc_ref[...] = jnp.zeros_like(acc_ref)

    nxt = jnp.minimum(i + 1, n_tiles - 1)
    for j in range(1, tm // MOE_SUB + 1):
        rows = j * MOE_SUB

        @pl.when(jnp.logical_and(used, nsub_ref[i] == j))
        def _(rows=rows):
            for u in range(rows_per_step):
                row_copy(nxt, k * rows_per_step + u, 1 - slot).start()
            wg = wg_ref[0, 0].astype(jnp.bfloat16)
            wu = wu_ref[0, 0].astype(jnp.bfloat16)
            wd = wd_ref[0, 0].astype(jnp.bfloat16)
            acc_ref[:rows] += _swiglu_chunk(h_ref[:rows], wg, wu, wd)

    @pl.when(k == nk - 1)
    def _():
        out_copy(i).start()

        @pl.when(i == n_tiles - 1)
        def _():
            out_copy(i).wait()

            @pl.when(used)
            def _():
                wait_rows(1 - slot)


def _moe_ffn(x2, gain, wg, wu, wd, layer, tile_expert, n_used, n_sub, row_src, tm, n_tiles):
    t, d = x2.shape
    f = wg.shape[-1]
    tf = min(256, f)
    nk = f // tf
    rows_per_step = -(-tm // nk)
    while (rows_per_step * nk) % DMA_UNROLL:
        rows_per_step += 1
    buf_rows = rows_per_step * nk

    def wmap_col(i, k, te, nu, ns, src):
        return (layer, te[i], 0, jnp.where(i < nu[0], k, nk - 1))

    def wmap_row(i, k, te, nu, ns, src):
        return (layer, te[i], jnp.where(i < nu[0], k, nk - 1), 0)

    return pl.pallas_call(
        functools.partial(_moe_kernel, tm=tm, rows_per_step=rows_per_step, n_tiles=n_tiles),
        grid_spec=pltpu.PrefetchScalarGridSpec(
            num_scalar_prefetch=4,
            grid=(n_tiles, nk),
            in_specs=[
                pl.BlockSpec(memory_space=pl.ANY),
                pl.BlockSpec((1, d), lambda i, k, te, nu, ns, src: (0, 0)),
                pl.BlockSpec((1, 1, d, tf), wmap_col),
                pl.BlockSpec((1, 1, d, tf), wmap_col),
                pl.BlockSpec((1, 1, tf, d), wmap_row),
            ],
            out_specs=pl.BlockSpec(memory_space=pl.ANY),
            scratch_shapes=[
                pltpu.VMEM((2, buf_rows, d), jnp.float32),
                pltpu.VMEM((tm, d), jnp.float32),
                pltpu.VMEM((tm, d), jnp.bfloat16),
                pltpu.SemaphoreType.DMA((2,)),
                pltpu.SemaphoreType.DMA(()),
            ],
        ),
        out_shape=jax.ShapeDtypeStruct((n_tiles * tm, d), jnp.float32),
        compiler_params=_params("arbitrary", "arbitrary"),
        name="moe_ffn",
    )(tile_expert, n_used, n_sub, row_src, x2, gain, wg, wu, wd)


def _combine_kernel(pos_ref, x_ref, gate_ref, fg_ref, y_hbm, o_ref, buf_ref, sem, *, tc, t, final):
    i = pl.program_id(0)
    n = pl.num_programs(0)
    slot = i % 2

    def issue(tile, s):
        def body(r8, c):
            for u in range(DMA_UNROLL):
                r = r8 * DMA_UNROLL + u
                for kk in range(TOP_K):
                    pltpu.make_async_copy(y_hbm.at[pl.ds(pos_ref[kk * t + tile * tc + r], 1)],
                                          buf_ref.at[s, kk, pl.ds(r, 1)], sem.at[s]).start()
            return c
        lax.fori_loop(0, tc // DMA_UNROLL, body, 0)

    @pl.when(i == 0)
    def _():
        issue(0, 0)

    @pl.when(i + 1 < n)
    def _():
        issue(i + 1, 1 - slot)

    for kk in range(TOP_K):
        pltpu.make_async_copy(buf_ref.at[slot, kk], buf_ref.at[slot, kk], sem.at[slot]).wait()
    gate = gate_ref[...]
    y = x_ref[...] + gate[:, 2:3] * buf_ref[slot, 0] + gate[:, 3:4] * buf_ref[slot, 1]
    if final:
        y = _rms(y, fg_ref[...])
    o_ref[...] = y


def _combine(x2, routing, pos, y_sorted, final_gain, final):
    t, d = x2.shape
    tc = min(256, t)
    return pl.pallas_call(
        functools.partial(_combine_kernel, tc=tc, t=t, final=final),
        grid_spec=pltpu.PrefetchScalarGridSpec(
            num_scalar_prefetch=1,
            grid=(t // tc,),
            in_specs=[
                pl.BlockSpec((tc, d), lambda i, p: (i, 0)),
                pl.BlockSpec((tc, LANES), lambda i, p: (i, 0)),
                pl.BlockSpec((1, d), lambda i, p: (0, 0)),
                pl.BlockSpec(memory_space=pl.ANY),
            ],
            out_specs=pl.BlockSpec((tc, d), lambda i, p: (i, 0)),
            scratch_shapes=[
                pltpu.VMEM((2, TOP_K, tc, d), jnp.float32),
                pltpu.SemaphoreType.DMA((2,)),
            ],
        ),
        out_shape=jax.ShapeDtypeStruct((t, d), jnp.float32),
        compiler_params=_params("arbitrary"),
        name="combine",
    )(pos, x2, routing, final_gain, y_sorted)


def _final_norm_kernel(x_ref, g_ref, o_ref):
    o_ref[...] = _rms(x_ref[...], g_ref[...])


def _final_norm(x2, gain):
    t, d = x2.shape
    tm = min(1024, t)
    return pl.pallas_call(
        _final_norm_kernel,
        grid=(t // tm,),
        in_specs=[pl.BlockSpec((tm, d), lambda i: (i, 0)), pl.BlockSpec((1, d), lambda i: (0, 0))],
        out_specs=pl.BlockSpec((tm, d), lambda i: (i, 0)),
        out_shape=jax.ShapeDtypeStruct((t, d), jnp.float32),
        compiler_params=_params("arbitrary"),
        name="final_norm",
    )(x2, gain)


def _t5_bucket(rel):
    half = N_BUCKETS // 2
    max_exact = half // 2
    side = jnp.where(rel > 0, half, 0)
    n = jnp.abs(rel)
    nf = jnp.maximum(n, 1).astype(jnp.float32)
    large = max_exact + (jnp.log(nf / max_exact) / math.log(MAX_DISTANCE / max_exact)
                         * (half - max_exact)).astype(jnp.int32)
    large = jnp.minimum(large, half - 1)
    return side + jnp.where(n < max_exact, n, large)


def _window_bias(rel_table):
    qi = jnp.arange(BLOCK, dtype=jnp.int32)[None, :, None]
    si = jnp.arange(3 * BLOCK, dtype=jnp.int32)[None, None, :]
    off = (jnp.arange(3, dtype=jnp.int32) * BLOCK)[:, None, None]
    rel = si - off - qi
    bucket = _t5_bucket(rel)[:, None]
    table = rel_table.astype(jnp.float32).T[None, :, :, None, None]
    bias = jnp.full((3, rel_table.shape[1], BLOCK, 3 * BLOCK), NEG_INF, jnp.float32)
    valid = (jnp.abs(rel) <= WINDOW)[:, None]
    for b in range(N_BUCKETS):
        bias = jnp.where(valid & (bucket == b), table[:, :, b], bias)
    return bias


def _rope_tables(seq):
    pos = jnp.arange(seq, dtype=jnp.int32)
    row = (pos // GRID_W).astype(jnp.float32)
    col = (pos % GRID_W).astype(jnp.float32)
    inv = 1.0 / (ROPE_THETA ** (jnp.arange(0, ROPE_AXIS_DIM, 2, dtype=jnp.float32) / ROPE_AXIS_DIM))
    ar = row[:, None] * inv[None, :]
    ac = col[:, None] * inv[None, :]
    cos = jnp.concatenate([jnp.cos(ar), jnp.cos(ar), jnp.cos(ac), jnp.cos(ac)], axis=-1)
    sin = jnp.concatenate([-jnp.sin(ar), jnp.sin(ar), -jnp.sin(ac), jnp.sin(ac)], axis=-1)
    return cos, sin


def _routing_plan(routing, n_experts, tm, n_tiles):
    t = routing.shape[0]
    experts = jnp.concatenate([routing[:, 0], routing[:, 1]]).astype(jnp.int32)
    onehot = (experts[:, None] == jnp.arange(n_experts, dtype=jnp.int32)[None, :]).astype(jnp.int32)
    csum = jnp.cumsum(onehot, axis=0)
    rank = jnp.sum(onehot * csum, axis=1) - 1
    counts = csum[-1]
    tiles_per = (counts + tm - 1) // tm
    tile_end = jnp.cumsum(tiles_per)
    tile_start = tile_end - tiles_per
    pos = (jnp.sum(onehot * (tile_start * tm)[None, :], axis=1) + rank).astype(jnp.int32)
    tok = jnp.arange(TOP_K * t, dtype=jnp.int32) % t
    row_src = jnp.zeros((n_tiles * tm,), jnp.int32).at[pos].set(tok)
    n_used = tile_end[-1].astype(jnp.int32)
    tile_id = jnp.minimum(jnp.arange(n_tiles, dtype=jnp.int32), n_used - 1)
    tile_expert = jnp.sum((tile_id[:, None] >= tile_end[None, :]).astype(jnp.int32), axis=1)
    filled = jnp.clip(counts[tile_expert] - (tile_id - tile_start[tile_expert]) * tm, 1, tm)
    n_sub = (filled + MOE_SUB - 1) // MOE_SUB
    return (tile_expert.astype(jnp.int32), n_used.reshape(1), n_sub.astype(jnp.int32),
            row_src, pos)


def kernel(x, norm_mix, w_in, w_out, sink_a, rel_bias, q_norm_b, k_norm_b, norm_ffn,
           w_dense_gate, w_dense_up, w_dense_down, w_router, w_exp_gate, w_exp_up,
           w_exp_down, norm_final):
    batch, seq, d = x.shape
    depth = w_in.shape[0]
    n_experts = w_router.shape[-1]
    t = batch * seq
    bf16 = jnp.bfloat16
    x2 = x.reshape(t, d)
    cos, sin = _rope_tables(seq)
    bias = _window_bias(rel_bias)
    moe_tm = min(1024, t)
    moe_tiles = (TOP_K * t) // moe_tm + n_experts
    for l in range(depth):
        proj = _proj(x2, norm_mix[l][None], w_in[l].astype(bf16), cos, sin,
                     q_norm_b[l][None], k_norm_b[l][None], seq)
        out_a = _window_attention(proj, sink_a[l], bias, batch, seq)
        out_b = _global_attention(proj, batch, seq)
        x2 = _outproj(out_a, out_b, w_out[l].astype(bf16), x2)
        last = l == depth - 1
        i = l // 2
        if l % 2 == 0:
            x2 = _ffn_dense(x2, norm_ffn[l][None], w_dense_gate[i].astype(bf16),
                            w_dense_up[i].astype(bf16), w_dense_down[i].astype(bf16))
            if last:
                x2 = _final_norm(x2, norm_final[None])
        else:
            w_pad = jnp.pad(w_router[i], ((0, 0), (0, LANES - n_experts)))
            routing = _router(x2, norm_ffn[l][None], w_pad, n_experts)
            tile_expert, n_used, n_sub, row_src, pos = _routing_plan(
                routing, n_experts, moe_tm, moe_tiles)
            y_sorted = _moe_ffn(x2, norm_ffn[l][None], w_exp_gate, w_exp_up, w_exp_down, i,
                                tile_expert, n_used, n_sub, row_src, moe_tm, moe_tiles)
            x2 = _combine(x2, routing, pos, y_sorted, norm_final[None], last)
    return x2.reshape(batch, seq, d)
```

```python
import functools
import math

import jax
import jax.numpy as jnp
from jax import lax
from jax.experimental import pallas as pl
from jax.experimental.pallas import tpu as pltpu

HEAD_DIM = 128
N_HEADS_A = 8
N_KV_A = 2
N_HEADS_B = 8
N_KV_B = 2
GROUP_A = N_HEADS_A // N_KV_A
GROUP_B = N_HEADS_B // N_KV_B
Q_A = N_HEADS_A * HEAD_DIM
KV_A = N_KV_A * HEAD_DIM
Q_B = N_HEADS_B * HEAD_DIM
KV_B = N_KV_B * HEAD_DIM
HALF_IN = Q_A + 2 * KV_A
WINDOW = 128
BLOCK = 128
GRID_W = 64
ROPE_THETA = 10000.0
ROPE_AXIS_DIM = HEAD_DIM // 2
N_BUCKETS = 32
MAX_DISTANCE = 128
TOP_K = 2
EPS = 1e-6
NEG_INF = -1e30
LANES = 128
QK_SCALE = HEAD_DIM ** -0.5
LOG2_E = math.log2(math.e)
VMEM_LIMIT = 56 * 1024 * 1024
MOE_SUB = 256
DMA_UNROLL = 8


def _params(*sem):
    return pltpu.CompilerParams(dimension_semantics=sem, vmem_limit_bytes=VMEM_LIMIT)


def _rms(x, g):
    return x * lax.rsqrt(jnp.mean(x * x, axis=-1, keepdims=True) + EPS) * g


def _dot(a, b):
    return jnp.dot(a, b, preferred_element_type=jnp.float32)


def _dot_t(a, b):
    return lax.dot_general(a, b, (((1,), (1,)), ((), ())), preferred_element_type=jnp.float32)


def _rope(x, cos, sin_signed):
    lane = lax.broadcasted_iota(jnp.int32, x.shape, 1)
    swapped = jnp.where((lane & 32) == 0,
                        pltpu.roll(x, LANES - 32, axis=1), pltpu.roll(x, 32, axis=1))
    return x * cos + swapped * sin_signed


def _proj_kernel(x_ref, g_ref, w_ref, cos_ref, sin_ref, qg_ref, kg_ref, o_ref, wbf_ref, *,
                 q_scale_b):
    @pl.when(pl.program_id(1) == 0)
    def _():
        wbf_ref[...] = w_ref[...].astype(wbf_ref.dtype)

    h = _rms(x_ref[...], g_ref[...]).astype(jnp.bfloat16)
    group = pl.program_id(0)
    pw = 2 * HEAD_DIM
    n_pieces = HALF_IN // pw

    def piece(c):
        return _dot(h, wbf_ref[:, c * pw:(c + 1) * pw])

    def pipelined(epilogue):
        y = piece(0)
        for c in range(n_pieces):
            nxt = piece(c + 1) if c + 1 < n_pieces else None
            o_ref[:, c * pw:(c + 1) * pw] = epilogue(c, y).astype(o_ref.dtype)
            y = nxt

    @pl.when(group == 0)
    def _():
        pipelined(lambda c, y: y * QK_SCALE if c * pw < Q_A else y)

    @pl.when(group == 1)
    def _():
        cos = cos_ref[...]
        sin = sin_ref[...]

        def epilogue(c, y):
            if c * pw >= Q_B + KV_B:
                return y
            is_q = c * pw < Q_B
            gain = qg_ref[...] if is_q else kg_ref[...]
            halves = [_rms(y[:, j * HEAD_DIM:(j + 1) * HEAD_DIM], gain) for j in range(2)]
            halves = [_rope(v, cos, sin) for v in halves]
            if is_q:
                halves = [v * q_scale_b for v in halves]
            return jnp.concatenate(halves, axis=1)

        pipelined(epilogue)


def _proj(x2, gain, w, cos, sin, qg, kg, seq):
    t, d = x2.shape
    tm = min(512, seq)
    spt = seq // tm
    return pl.pallas_call(
        functools.partial(_proj_kernel, q_scale_b=QK_SCALE * LOG2_E),
        grid=(2, t // tm),
        in_specs=[
            pl.BlockSpec((tm, d), lambda j, i: (i, 0)),
            pl.BlockSpec((1, d), lambda j, i: (0, 0)),
            pl.BlockSpec((d, HALF_IN), lambda j, i: (0, j)),
            pl.BlockSpec((tm, HEAD_DIM), lambda j, i: (i % spt, 0)),
            pl.BlockSpec((tm, HEAD_DIM), lambda j, i: (i % spt, 0)),
            pl.BlockSpec((1, HEAD_DIM), lambda j, i: (0, 0)),
            pl.BlockSpec((1, HEAD_DIM), lambda j, i: (0, 0)),
        ],
        out_specs=pl.BlockSpec((tm, HALF_IN), lambda j, i: (i, j)),
        out_shape=jax.ShapeDtypeStruct((t, 2 * HALF_IN), jnp.bfloat16),
        scratch_shapes=[pltpu.VMEM((d, HALF_IN), jnp.bfloat16)],
        compiler_params=_params("arbitrary", "arbitrary"),
        name="proj",
    )(x2, gain, w, cos, sin, qg, kg)


def _win_kernel(sink_ref, q_ref, k_ref, v_ref, bias_ref, o_ref, *, qb, nb, seq):
    step = pl.program_id(1)
    for jb in range(qb):
        n = step * qb + jb
        ws = pl.multiple_of(jnp.clip((n - 1) * BLOCK, 0, seq - 3 * BLOCK), BLOCK)
        variant = jnp.where(n == 0, 0, jnp.where(n == nb - 1, 2, 1))
        rows = slice(jb * BLOCK, (jb + 1) * BLOCK)
        heads = range(N_HEADS_A)
        hcols = [slice(hd * HEAD_DIM, (hd + 1) * HEAD_DIM) for hd in heads]
        kws = [k_ref[pl.ds(ws, 3 * BLOCK), kvh * HEAD_DIM:(kvh + 1) * HEAD_DIM] for kvh in range(N_KV_A)]
        vws = [v_ref[pl.ds(ws, 3 * BLOCK), kvh * HEAD_DIM:(kvh + 1) * HEAD_DIM] for kvh in range(N_KV_A)]
        ss = [_dot_t(q_ref[rows, hcols[hd]], kws[hd // GROUP_A]) + bias_ref[variant, hd] for hd in heads]
        ms = [jnp.maximum(jnp.max(ss[hd], axis=-1, keepdims=True), sink_ref[hd]) for hd in heads]
        ps = [jnp.exp(ss[hd] - ms[hd]) for hd in heads]
        ds = [jnp.sum(ps[hd], axis=-1, keepdims=True) + jnp.exp(sink_ref[hd] - ms[hd]) for hd in heads]
        os_ = [_dot(ps[hd].astype(jnp.bfloat16), vws[hd // GROUP_A]) for hd in heads]
        for hd in heads:
            o_ref[rows, hcols[hd]] = (os_[hd] / ds[hd]).astype(o_ref.dtype)


def _window_attention(proj, sink, bias, batch, seq):
    nb = seq // BLOCK
    qb = min(4, nb)
    spb = seq // (qb * BLOCK)
    kern = functools.partial(_win_kernel, qb=qb, nb=nb, seq=seq)
    return pl.pallas_call(
        kern,
        grid_spec=pltpu.PrefetchScalarGridSpec(
            num_scalar_prefetch=1,
            grid=(batch, spb),
            in_specs=[
                pl.BlockSpec((qb * BLOCK, Q_A), lambda b, s, sk: (b * spb + s, 0)),
                pl.BlockSpec((seq, KV_A), lambda b, s, sk: (b, Q_A // KV_A)),
                pl.BlockSpec((seq, KV_A), lambda b, s, sk: (b, Q_A // KV_A + 1)),
                pl.BlockSpec(bias.shape, lambda b, s, sk: (0, 0, 0, 0)),
            ],
            out_specs=pl.BlockSpec((qb * BLOCK, Q_A), lambda b, s, sk: (b * spb + s, 0)),
        ),
        out_shape=jax.ShapeDtypeStruct((batch * seq, Q_A), jnp.bfloat16),
        compiler_params=_params("arbitrary", "arbitrary"),
        name="window_attention",
    )(sink, proj, proj, proj, bias)


def _glob_kernel(q_ref, k_ref, v_ref, o_ref):
    k = k_ref[...]
    v = v_ref[...]
    heads = range(GROUP_B)
    cols = [slice(g * HEAD_DIM, (g + 1) * HEAD_DIM) for g in heads]
    ss = [_dot_t(q_ref[:, cols[g]], k) for g in heads]
    ms = [jnp.max(ss[g], axis=-1, keepdims=True) for g in heads]
    ps = [jnp.exp2(ss[g] - ms[g]) for g in heads]
    ds = [jnp.sum(ps[g], axis=-1, keepdims=True) for g in heads]
    os_ = [_dot(ps[g].astype(jnp.bfloat16), v) for g in heads]
    for g in heads:
        o_ref[:, cols[g]] = (os_[g] / ds[g]).astype(o_ref.dtype)


def _global_attention(proj, batch, seq):
    tq = min(256, seq)
    spb = seq // tq
    gw = GROUP_B * HEAD_DIM
    q_blk0 = HALF_IN // gw
    k_blk0 = (HALF_IN + Q_B) // HEAD_DIM
    v_blk0 = (HALF_IN + Q_B + KV_B) // HEAD_DIM
    return pl.pallas_call(
        _glob_kernel,
        grid=(batch, N_KV_B, spb),
        in_specs=[
            pl.BlockSpec((tq, gw), lambda b, h, s: (b * spb + s, q_blk0 + h)),
            pl.BlockSpec((seq, HEAD_DIM), lambda b, h, s: (b, k_blk0 + h)),
            pl.BlockSpec((seq, HEAD_DIM), lambda b, h, s: (b, v_blk0 + h)),
        ],
        out_specs=pl.BlockSpec((tq, gw), lambda b, h, s: (b * spb + s, h)),
        out_shape=jax.ShapeDtypeStruct((batch * seq, Q_B), jnp.bfloat16),
        compiler_params=_params("arbitrary", "arbitrary", "arbitrary"),
        name="global_attention",
    )(proj, proj, proj)


def _outproj_kernel(a_ref, b_ref, wa_ref, wb_ref, x_ref, o_ref):
    o_ref[...] = x_ref[...] + _dot(a_ref[...], wa_ref[...]) + _dot(b_ref[...], wb_ref[...])


def _outproj(out_a, out_b, w_bf16, x2):
    t, d = x2.shape
    tm = min(512, t)
    tn = min(2048, d)
    assert Q_A == Q_B
    return pl.pallas_call(
        _outproj_kernel,
        grid=(t // tm, d // tn),
        in_specs=[
            pl.BlockSpec((tm, Q_A), lambda i, j: (i, 0)),
            pl.BlockSpec((tm, Q_B), lambda i, j: (i, 0)),
            pl.BlockSpec((Q_A, tn), lambda i, j: (0, j)),
            pl.BlockSpec((Q_B, tn), lambda i, j: (1, j)),
            pl.BlockSpec((tm, tn), lambda i, j: (i, j)),
        ],
        out_specs=pl.BlockSpec((tm, tn), lambda i, j: (i, j)),
        out_shape=jax.ShapeDtypeStruct((t, d), jnp.float32),
        compiler_params=_params("arbitrary", "arbitrary"),
        name="outproj",
    )(out_a, out_b, w_bf16, w_bf16, x2)


def _swiglu_chunk(h, wg, wu, wd):
    a = _dot(h, wg)
    b = _dot(h, wu)
    act = (a * jax.nn.sigmoid(a) * b).astype(jnp.bfloat16)
    return _dot(act, wd)


def _ffn_kernel(x_ref, g_ref, wg_ref, wu_ref, wd_ref, o_ref, h_ref):
    k = pl.program_id(1)

    @pl.when(k == 0)
    def _():
        x = x_ref[...]
        h_ref[...] = _rms(x, g_ref[...]).astype(h_ref.dtype)
        o_ref[...] = x

    o_ref[...] += _swiglu_chunk(h_ref[...], wg_ref[...], wu_ref[...], wd_ref[...])


def _ffn_dense(x2, gain, wg, wu, wd):
    t, d = x2.shape
    f = wg.shape[1]
    tm = min(1024, t)
    tf = next(c for c in (256, 128) if f % c == 0)
    return pl.pallas_call(
        _ffn_kernel,
        grid=(t // tm, f // tf),
        in_specs=[
            pl.BlockSpec((tm, d), lambda i, k: (i, 0)),
            pl.BlockSpec((1, d), lambda i, k: (0, 0)),
            pl.BlockSpec((d, tf), lambda i, k: (0, k)),
            pl.BlockSpec((d, tf), lambda i, k: (0, k)),
            pl.BlockSpec((tf, d), lambda i, k: (k, 0)),
        ],
        out_specs=pl.BlockSpec((tm, d), lambda i, k: (i, 0)),
        out_shape=jax.ShapeDtypeStruct((t, d), jnp.float32),
        scratch_shapes=[pltpu.VMEM((tm, d), jnp.bfloat16)],
        compiler_params=_params("arbitrary", "arbitrary"),
        name="ffn_dense",
    )(x2, gain, wg, wu, wd)


def _router_kernel(x_ref, g_ref, w_ref, o_ref, *, n_experts):
    h = _rms(x_ref[...], g_ref[...])
    logits = jnp.dot(h, w_ref[...], preferred_element_type=jnp.float32,
                     precision=lax.Precision.HIGHEST)
    lane = lax.broadcasted_iota(jnp.int32, logits.shape, 1)
    logits = jnp.where(lane < n_experts, logits, -jnp.inf)
    v1 = jnp.max(logits, axis=-1, keepdims=True)
    i1 = jnp.min(jnp.where(logits == v1, lane, LANES), axis=-1, keepdims=True)
    rest = jnp.where(lane == i1, -jnp.inf, logits)
    v2 = jnp.max(rest, axis=-1, keepdims=True)
    i2 = jnp.min(jnp.where(rest == v2, lane, LANES), axis=-1, keepdims=True)
    e2 = jnp.exp(v2 - v1)
    g1 = 1.0 / (1.0 + e2)
    g2 = e2 / (1.0 + e2)
    o_ref[...] = jnp.where(lane == 0, i1.astype(jnp.float32),
                           jnp.where(lane == 1, i2.astype(jnp.float32),
                                     jnp.where(lane == 2, g1, jnp.where(lane == 3, g2, 0.0))))


def _router(x2, gain, w_pad, n_experts):
    t, d = x2.shape
    tm = min(512, t)
    return pl.pallas_call(
        functools.partial(_router_kernel, n_experts=n_experts),
        grid=(t // tm,),
        in_specs=[
            pl.BlockSpec((tm, d), lambda i: (i, 0)),
            pl.BlockSpec((1, d), lambda i: (0, 0)),
            pl.BlockSpec((d, LANES), lambda i: (0, 0)),
        ],
        out_specs=pl.BlockSpec((tm, LANES), lambda i: (i, 0)),
        out_shape=jax.ShapeDtypeStruct((t, LANES), jnp.float32),
        compiler_params=_params("arbitrary"),
        name="router",
    )(x2, gain, w_pad)


def _moe_kernel(te_ref, nu_ref, nsub_ref, src_ref, x_hbm, g_ref, wg_ref, wu_ref, wd_ref, o_hbm,
                xg_ref, acc_ref, h_ref, sem_in, sem_out, *, tm, rows_per_step, n_tiles):
    i = pl.program_id(0)
    k = pl.program_id(1)
    nk = pl.num_programs(1)
    n_used = nu_ref[0]
    used = i < n_used
    slot = i % 2
    buf_rows = xg_ref.shape[1]

    def row_copy(tile, r, s):
        tok = src_ref[tile * tm + jnp.minimum(r, tm - 1)]
        return pltpu.make_async_copy(x_hbm.at[pl.ds(tok, 1)], xg_ref.at[s, pl.ds(r, 1)],
                                     sem_in.at[s])

    def wait_rows(s):
        pltpu.make_async_copy(xg_ref.at[s], xg_ref.at[s], sem_in.at[s]).wait()

    def out_copy(tile):
        return pltpu.make_async_copy(
            acc_ref, o_hbm.at[pl.ds(pl.multiple_of(tile * tm, tm), tm)], sem_out)

    @pl.when(jnp.logical_and(i == 0, k == 0))
    def _():
        def start(r, c):
            for u in range(DMA_UNROLL):
                row_copy(0, r * DMA_UNROLL + u, 0).start()
            return c
        lax.fori_loop(0, buf_rows // DMA_UNROLL, start, 0)

    @pl.when(k == 0)
    def _():
        @pl.when(i <= n_used)
        def _():
            wait_rows(slot)

        @pl.when(used)
        def _():
            h_ref[...] = _rms(xg_ref[slot, pl.ds(0, tm), :], g_ref[...]).astype(h_ref.dtype)

        @pl.when(i > 0)
        def _():
            out_copy(i - 1).wait()
        acc_ref[...] = jnp.zeros_like(acc_ref)

    nxt = jnp.minimum(i + 1, n_tiles - 1)
    for j in range(1, tm // MOE_SUB + 1):
        rows = j * MOE_SUB

        @pl.when(jnp.logical_and(used, nsub_ref[i] == j))
        def _(rows=rows):
            for u in range(rows_per_step):
                row_copy(nxt, k * rows_per_step + u, 1 - slot).start()
            wg = wg_ref[0, 0].astype(jnp.bfloat16)
            wu = wu_ref[0, 0].astype(jnp.bfloat16)
            wd = wd_ref[0, 0].astype(jnp.bfloat16)
            acc_ref[:rows] += _swiglu_chunk(h_ref[:rows], wg, wu, wd)

    @pl.when(k == nk - 1)
    def _():
        out_copy(i).start()

        @pl.when(i == n_tiles - 1)
        def _():
            out_copy(i).wait()

            @pl.when(used)
            def _():
                wait_rows(1 - slot)


def _moe_ffn(x2, gain, wg, wu, wd, layer, tile_expert, n_used, n_sub, row_src, tm, n_tiles):
    t, d = x2.shape
    f = wg.shape[-1]
    tf = min(256, f)
    nk = f // tf
    rows_per_step = -(-tm // nk)
    while (rows_per_step * nk) % DMA_UNROLL:
        rows_per_step += 1
    buf_rows = rows_per_step * nk

    def wmap_col(i, k, te, nu, ns, src):
        return (layer, te[i], 0, jnp.where(i < nu[0], k, nk - 1))

    def wmap_row(i, k, te, nu, ns, src):
        return (layer, te[i], jnp.where(i < nu[0], k, nk - 1), 0)

    return pl.pallas_call(
        functools.partial(_moe_kernel, tm=tm, rows_per_step=rows_per_step, n_tiles=n_tiles),
        grid_spec=pltpu.PrefetchScalarGridSpec(
            num_scalar_prefetch=4,
            grid=(n_tiles, nk),
            in_specs=[
                pl.BlockSpec(memory_space=pl.ANY),
                pl.BlockSpec((1, d), lambda i, k, te, nu, ns, src: (0, 0)),
                pl.BlockSpec((1, 1, d, tf), wmap_col),
                pl.BlockSpec((1, 1, d, tf), wmap_col),
                pl.BlockSpec((1, 1, tf, d), wmap_row),
            ],
            out_specs=pl.BlockSpec(memory_space=pl.ANY),
            scratch_shapes=[
                pltpu.VMEM((2, buf_rows, d), jnp.float32),
                pltpu.VMEM((tm, d), jnp.float32),
                pltpu.VMEM((tm, d), jnp.bfloat16),
                pltpu.SemaphoreType.DMA((2,)),
                pltpu.SemaphoreType.DMA(()),
            ],
        ),
        out_shape=jax.ShapeDtypeStruct((n_tiles * tm, d), jnp.float32),
        compiler_params=_params("arbitrary", "arbitrary"),
        name="moe_ffn",
    )(tile_expert, n_used, n_sub, row_src, x2, gain, wg, wu, wd)


def _combine_kernel(pos_ref, x_ref, gate_ref, fg_ref, y_hbm, o_ref, buf_ref, sem, *, tc, t, final):
    i = pl.program_id(0)
    n = pl.num_programs(0)
    slot = i % 2

    def issue(tile, s):
        def body(r8, c):
            for u in range(DMA_UNROLL):
                r = r8 * DMA_UNROLL + u
                for kk in range(TOP_K):
                    pltpu.make_async_copy(y_hbm.at[pl.ds(pos_ref[kk * t + tile * tc + r], 1)],
                                          buf_ref.at[s, kk, pl.ds(r, 1)], sem.at[s]).start()
            return c
        lax.fori_loop(0, tc // DMA_UNROLL, body, 0)

    @pl.when(i == 0)
    def _():
        issue(0, 0)

    @pl.when(i + 1 < n)
    def _():
        issue(i + 1, 1 - slot)

    for kk in range(TOP_K):
        pltpu.make_async_copy(buf_ref.at[slot, kk], buf_ref.at[slot, kk], sem.at[slot]).wait()
    gate = gate_ref[...]
    y = x_ref[...] + gate[:, 2:3] * buf_ref[slot, 0] + gate[:, 3:4] * buf_ref[slot, 1]
    if final:
        y = _rms(y, fg_ref[...])
    o_ref[...] = y


def _combine(x2, routing, pos, y_sorted, final_gain, final):
    t, d = x2.shape
    tc = min(256, t)
    return pl.pallas_call(
        functools.partial(_combine_kernel, tc=tc, t=t, final=final),
        grid_spec=pltpu.PrefetchScalarGridSpec(
            num_scalar_prefetch=1,
            grid=(t // tc,),
            in_specs=[
                pl.BlockSpec((tc, d), lambda i, p: (i, 0)),
                pl.BlockSpec((tc, LANES), lambda i, p: (i, 0)),
                pl.BlockSpec((1, d), lambda i, p: (0, 0)),
                pl.BlockSpec(memory_space=pl.ANY),
            ],
            out_specs=pl.BlockSpec((tc, d), lambda i, p: (i, 0)),
            scratch_shapes=[
                pltpu.VMEM((2, TOP_K, tc, d), jnp.float32),
                pltpu.SemaphoreType.DMA((2,)),
            ],
        ),
        out_shape=jax.ShapeDtypeStruct((t, d), jnp.float32),
        compiler_params=_params("arbitrary"),
        name="combine",
    )(pos, x2, routing, final_gain, y_sorted)


def _final_norm_kernel(x_ref, g_ref, o_ref):
    o_ref[...] = _rms(x_ref[...], g_ref[...])


def _final_norm(x2, gain):
    t, d = x2.shape
    tm = min(1024, t)
    return pl.pallas_call(
        _final_norm_kernel,
        grid=(t // tm,),
        in_specs=[pl.BlockSpec((tm, d), lambda i: (i, 0)), pl.BlockSpec((1, d), lambda i: (0, 0))],
        out_specs=pl.BlockSpec((tm, d), lambda i: (i, 0)),
        out_shape=jax.ShapeDtypeStruct((t, d), jnp.float32),
        compiler_params=_params("arbitrary"),
        name="final_norm",
    )(x2, gain)


def _t5_bucket(rel):
    half = N_BUCKETS // 2
    max_exact = half // 2
    side = jnp.where(rel > 0, half, 0)
    n = jnp.abs(rel)
    nf = jnp.maximum(n, 1).astype(jnp.float32)
    large = max_exact + (jnp.log(nf / max_exact) / math.log(MAX_DISTANCE / max_exact)
                         * (half - max_exact)).astype(jnp.int32)
    large = jnp.minimum(large, half - 1)
    return side + jnp.where(n < max_exact, n, large)


def _window_bias(rel_table):
    qi = jnp.arange(BLOCK, dtype=jnp.int32)[None, :, None]
    si = jnp.arange(3 * BLOCK, dtype=jnp.int32)[None, None, :]
    off = (jnp.arange(3, dtype=jnp.int32) * BLOCK)[:, None, None]
    rel = si - off - qi
    bucket = _t5_bucket(rel)[:, None]
    table = rel_table.astype(jnp.float32).T[None, :, :, None, None]
    bias = jnp.full((3, rel_table.shape[1], BLOCK, 3 * BLOCK), NEG_INF, jnp.float32)
    valid = (jnp.abs(rel) <= WINDOW)[:, None]
    for b in range(N_BUCKETS):
        bias = jnp.where(valid & (bucket == b), table[:, :, b], bias)
    return bias


def _rope_tables(seq):
    pos = jnp.arange(seq, dtype=jnp.int32)
    row = (pos // GRID_W).astype(jnp.float32)
    col = (pos % GRID_W).astype(jnp.float32)
    inv = 1.0 / (ROPE_THETA ** (jnp.arange(0, ROPE_AXIS_DIM, 2, dtype=jnp.float32) / ROPE_AXIS_DIM))
    ar = row[:, None] * inv[None, :]
    ac = col[:, None] * inv[None, :]
    cos = jnp.concatenate([jnp.cos(ar), jnp.cos(ar), jnp.cos(ac), jnp.cos(ac)], axis=-1)
    sin = jnp.concatenate([-jnp.sin(ar), jnp.sin(ar), -jnp.sin(ac), jnp.sin(ac)], axis=-1)
    return cos, sin


def _routing_plan(routing, n_experts, tm, n_tiles):
    t = routing.shape[0]
    experts = jnp.concatenate([routing[:, 0], routing[:, 1]]).astype(jnp.int32)
    onehot = (experts[:, None] == jnp.arange(n_experts, dtype=jnp.int32)[None, :]).astype(jnp.int32)
    csum = jnp.cumsum(onehot, axis=0)
    rank = jnp.sum(onehot * csum, axis=1) - 1
    counts = csum[-1]
    tiles_per = (counts + tm - 1) // tm
    tile_end = jnp.cumsum(tiles_per)
    tile_start = tile_end - tiles_per
    pos = (jnp.sum(onehot * (tile_start * tm)[None, :], axis=1) + rank).astype(jnp.int32)
    tok = jnp.arange(TOP_K * t, dtype=jnp.int32) % t
    row_src = jnp.zeros((n_tiles * tm,), jnp.int32).at[pos].set(tok)
    n_used = tile_end[-1].astype(jnp.int32)
    tile_id = jnp.minimum(jnp.arange(n_tiles, dtype=jnp.int32), n_used - 1)
    tile_expert = jnp.sum((tile_id[:, None] >= tile_end[None, :]).astype(jnp.int32), axis=1)
    filled = jnp.clip(counts[tile_expert] - (tile_id - tile_start[tile_expert]) * tm, 1, tm)
    n_sub = (filled + MOE_SUB - 1) // MOE_SUB
    return (tile_expert.astype(jnp.int32), n_used.reshape(1), n_sub.astype(jnp.int32),
            row_src, pos)


def kernel(x, norm_mix, w_in, w_out, sink_a, rel_bias, q_norm_b, k_norm_b, norm_ffn,
           w_dense_gate, w_dense_up, w_dense_down, w_router, w_exp_gate, w_exp_up,
           w_exp_down, norm_final):
    batch, seq, d = x.shape
    depth = w_in.shape[0]
    n_experts = w_router.shape[-1]
    t = batch * seq
    bf16 = jnp.bfloat16
    x2 = x.reshape(t, d)
    cos, sin = _rope_tables(seq)
    bias = _window_bias(rel_bias)
    moe_tm = min(1024, t)
    moe_tiles = (TOP_K * t) // moe_tm + n_experts
    for l in range(depth):
        proj = _proj(x2, norm_mix[l][None], w_in[l], cos, sin,
                     q_norm_b[l][None], k_norm_b[l][None], seq)
        out_a = _window_attention(proj, sink_a[l], bias, batch, seq)
        out_b = _global_attention(proj, batch, seq)
        x2 = _outproj(out_a, out_b, w_out[l].astype(bf16), x2)
        last = l == depth - 1
        i = l // 2
        if l % 2 == 0:
            x2 = _ffn_dense(x2, norm_ffn[l][None], w_dense_gate[i].astype(bf16),
                            w_dense_up[i].astype(bf16), w_dense_down[i].astype(bf16))
            if last:
                x2 = _final_norm(x2, norm_final[None])
        else:
            w_pad = jnp.pad(w_router[i], ((0, 0), (0, LANES - n_experts)))
            routing = _router(x2, norm_ffn[l][None], w_pad, n_experts)
            tile_expert, n_used, n_sub, row_src, pos = _routing_plan(
                routing, n_experts, moe_tm, moe_tiles)
            y_sorted = _moe_ffn(x2, norm_ffn[l][None], w_exp_gate, w_exp_up, w_exp_down, i,
                                tile_expert, n_used, n_sub, row_src, moe_tm, moe_tiles)
            x2 = _combine(x2, routing, pos, y_sorted, norm_final[None], last)
    return x2.reshape(batch, seq, d)
```

```python
import functools
import math

import jax
import jax.numpy as jnp
from jax import lax
from jax.experimental import pallas as pl
from jax.experimental.pallas import tpu as pltpu

HEAD_DIM = 128
N_HEADS_A = 8
N_KV_A = 2
N_HEADS_B = 8
N_KV_B = 2
GROUP_A = N_HEADS_A // N_KV_A
GROUP_B = N_HEADS_B // N_KV_B
Q_A = N_HEADS_A * HEAD_DIM
KV_A = N_KV_A * HEAD_DIM
Q_B = N_HEADS_B * HEAD_DIM
KV_B = N_KV_B * HEAD_DIM
HALF_IN = Q_A + 2 * KV_A
WINDOW = 128
BLOCK = 128
GRID_W = 64
ROPE_THETA = 10000.0
ROPE_AXIS_DIM = HEAD_DIM // 2
N_BUCKETS = 32
MAX_DISTANCE = 128
TOP_K = 2
EPS = 1e-6
NEG_INF = -1e30
LANES = 128
QK_SCALE = HEAD_DIM ** -0.5
LOG2_E = math.log2(math.e)
VMEM_LIMIT = 56 * 1024 * 1024
MOE_SUB = 256
DMA_UNROLL = 8


def _params(*sem):
    return pltpu.CompilerParams(dimension_semantics=sem, vmem_limit_bytes=VMEM_LIMIT)


def _rms(x, g):
    return x * lax.rsqrt(jnp.mean(x * x, axis=-1, keepdims=True) + EPS) * g


def _dot(a, b):
    return jnp.dot(a, b, preferred_element_type=jnp.float32)


def _dot_t(a, b):
    return lax.dot_general(a, b, (((1,), (1,)), ((), ())), preferred_element_type=jnp.float32)


def _rope(x, cos, sin_signed):
    lane = lax.broadcasted_iota(jnp.int32, x.shape, 1)
    swapped = jnp.where((lane & 32) == 0,
                        pltpu.roll(x, LANES - 32, axis=1), pltpu.roll(x, 32, axis=1))
    return x * cos + swapped * sin_signed


def _proj_kernel(x_ref, g_ref, w_ref, cos_ref, sin_ref, qg_ref, kg_ref, o_ref, wbf_ref, *,
                 q_scale_b):
    @pl.when(pl.program_id(1) == 0)
    def _():
        wbf_ref[...] = w_ref[...].astype(wbf_ref.dtype)

    h = _rms(x_ref[...], g_ref[...]).astype(jnp.bfloat16)
    group = pl.program_id(0)
    pw = 2 * HEAD_DIM
    n_pieces = HALF_IN // pw

    def piece(c):
        return _dot(h, wbf_ref[:, c * pw:(c + 1) * pw])

    def pipelined(epilogue):
        y = piece(0)
        for c in range(n_pieces):
            nxt = piece(c + 1) if c + 1 < n_pieces else None
            o_ref[:, c * pw:(c + 1) * pw] = epilogue(c, y).astype(o_ref.dtype)
            y = nxt

    @pl.when(group == 0)
    def _():
        pipelined(lambda c, y: y * QK_SCALE if c * pw < Q_A else y)

    @pl.when(group == 1)
    def _():
        cos = cos_ref[...]
        sin = sin_ref[...]

        def epilogue(c, y):
            if c * pw >= Q_B + KV_B:
                return y
            is_q = c * pw < Q_B
            gain = qg_ref[...] if is_q else kg_ref[...]
            halves = [_rms(y[:, j * HEAD_DIM:(j + 1) * HEAD_DIM], gain) for j in range(2)]
            halves = [_rope(v, cos, sin) for v in halves]
            if is_q:
                halves = [v * q_scale_b for v in halves]
            return jnp.concatenate(halves, axis=1)

        pipelined(epilogue)


def _proj(x2, gain, w, cos, sin, qg, kg, seq):
    t, d = x2.shape
    tm = min(512, seq)
    spt = seq // tm
    return pl.pallas_call(
        functools.partial(_proj_kernel, q_scale_b=QK_SCALE * LOG2_E),
        grid=(2, t // tm),
        in_specs=[
            pl.BlockSpec((tm, d), lambda j, i: (i, 0)),
            pl.BlockSpec((1, d), lambda j, i: (0, 0)),
            pl.BlockSpec((d, HALF_IN), lambda j, i: (0, j)),
            pl.BlockSpec((tm, HEAD_DIM), lambda j, i: (i % spt, 0)),
            pl.BlockSpec((tm, HEAD_DIM), lambda j, i: (i % spt, 0)),
            pl.BlockSpec((1, HEAD_DIM), lambda j, i: (0, 0)),
            pl.BlockSpec((1, HEAD_DIM), lambda j, i: (0, 0)),
        ],
        out_specs=pl.BlockSpec((tm, HALF_IN), lambda j, i: (i, j)),
        out_shape=jax.ShapeDtypeStruct((t, 2 * HALF_IN), jnp.bfloat16),
        scratch_shapes=[pltpu.VMEM((d, HALF_IN), jnp.bfloat16)],
        compiler_params=_params("arbitrary", "arbitrary"),
        name="proj",
    )(x2, gain, w, cos, sin, qg, kg)


def _win_kernel(sink_ref, q_ref, k_ref, v_ref, bias_ref, o_ref, *, qb, nb, seq):
    step = pl.program_id(1)
    for jb in range(qb):
        n = step * qb + jb
        ws = pl.multiple_of(jnp.clip((n - 1) * BLOCK, 0, seq - 3 * BLOCK), BLOCK)
        variant = jnp.where(n == 0, 0, jnp.where(n == nb - 1, 2, 1))
        rows = slice(jb * BLOCK, (jb + 1) * BLOCK)
        heads = range(N_HEADS_A)
        hcols = [slice(hd * HEAD_DIM, (hd + 1) * HEAD_DIM) for hd in heads]
        kws = [k_ref[pl.ds(ws, 3 * BLOCK), kvh * HEAD_DIM:(kvh + 1) * HEAD_DIM] for kvh in range(N_KV_A)]
        vws = [v_ref[pl.ds(ws, 3 * BLOCK), kvh * HEAD_DIM:(kvh + 1) * HEAD_DIM] for kvh in range(N_KV_A)]
        ss = [_dot_t(q_ref[rows, hcols[hd]], kws[hd // GROUP_A]) + bias_ref[variant, hd] for hd in heads]
        ms = [jnp.maximum(jnp.max(ss[hd], axis=-1, keepdims=True), sink_ref[hd]) for hd in heads]
        ps = [jnp.exp(ss[hd] - ms[hd]) for hd in heads]
        ds = [jnp.sum(ps[hd], axis=-1, keepdims=True) + jnp.exp(sink_ref[hd] - ms[hd]) for hd in heads]
        os_ = [_dot(ps[hd].astype(jnp.bfloat16), vws[hd // GROUP_A]) for hd in heads]
        for hd in heads:
            o_ref[rows, hcols[hd]] = (os_[hd] / ds[hd]).astype(o_ref.dtype)


def _window_attention(proj, sink, bias, batch, seq):
    nb = seq // BLOCK
    qb = min(4, nb)
    spb = seq // (qb * BLOCK)
    kern = functools.partial(_win_kernel, qb=qb, nb=nb, seq=seq)
    return pl.pallas_call(
        kern,
        grid_spec=pltpu.PrefetchScalarGridSpec(
            num_scalar_prefetch=1,
            grid=(batch, spb),
            in_specs=[
                pl.BlockSpec((qb * BLOCK, Q_A), lambda b, s, sk: (b * spb + s, 0)),
                pl.BlockSpec((seq, KV_A), lambda b, s, sk: (b, Q_A // KV_A)),
                pl.BlockSpec((seq, KV_A), lambda b, s, sk: (b, Q_A // KV_A + 1)),
                pl.BlockSpec(bias.shape, lambda b, s, sk: (0, 0, 0, 0)),
            ],
            out_specs=pl.BlockSpec((qb * BLOCK, Q_A), lambda b, s, sk: (b * spb + s, 0)),
        ),
        out_shape=jax.ShapeDtypeStruct((batch * seq, Q_A), jnp.bfloat16),
        compiler_params=_params("arbitrary", "arbitrary"),
        name="window_attention",
    )(sink, proj, proj, proj, bias)


def _glob_kernel(q0_ref, q1_ref, k_ref, v_ref, o_ref):
    q_refs = (q0_ref, q1_ref)

    def scores(hd):
        kv, g = divmod(hd, GROUP_B)
        return _dot_t(q_refs[kv][:, g * HEAD_DIM:(g + 1) * HEAD_DIM],
                      k_ref[:, kv * HEAD_DIM:(kv + 1) * HEAD_DIM])

    s_next = scores(0)
    for hd in range(N_HEADS_B):
        s = s_next
        if hd + 1 < N_HEADS_B:
            s_next = scores(hd + 1)
        kv = hd // GROUP_B
        m = jnp.max(s, axis=-1, keepdims=True)
        p = jnp.exp2(s - m)
        denom = jnp.sum(p, axis=-1, keepdims=True)
        o = _dot(p.astype(jnp.bfloat16), v_ref[:, kv * HEAD_DIM:(kv + 1) * HEAD_DIM]) / denom
        o_ref[:, hd * HEAD_DIM:(hd + 1) * HEAD_DIM] = o.astype(o_ref.dtype)


def _global_attention(proj, batch, seq):
    tq = min(256, seq)
    spb = seq // tq
    gw = GROUP_B * HEAD_DIM
    return pl.pallas_call(
        _glob_kernel,
        grid=(batch, spb),
        in_specs=[
            pl.BlockSpec((tq, gw), lambda b, s: (b * spb + s, HALF_IN // gw)),
            pl.BlockSpec((tq, gw), lambda b, s: (b * spb + s, HALF_IN // gw + 1)),
            pl.BlockSpec((seq, KV_B), lambda b, s: (b, (HALF_IN + Q_B) // KV_B)),
            pl.BlockSpec((seq, KV_B), lambda b, s: (b, (HALF_IN + Q_B) // KV_B + 1)),
        ],
        out_specs=pl.BlockSpec((tq, Q_B), lambda b, s: (b * spb + s, 0)),
        out_shape=jax.ShapeDtypeStruct((batch * seq, Q_B), jnp.bfloat16),
        compiler_params=_params("arbitrary", "arbitrary"),
        name="global_attention",
    )(proj, proj, proj, proj)


def _outproj_kernel(a_ref, b_ref, wa_ref, wb_ref, x_ref, o_ref):
    o_ref[...] = x_ref[...] + _dot(a_ref[...], wa_ref[...]) + _dot(b_ref[...], wb_ref[...])


def _outproj(out_a, out_b, w_bf16, x2):
    t, d = x2.shape
    tm = min(512, t)
    tn = min(2048, d)
    assert Q_A == Q_B
    return pl.pallas_call(
        _outproj_kernel,
        grid=(t // tm, d // tn),
        in_specs=[
            pl.BlockSpec((tm, Q_A), lambda i, j: (i, 0)),
            pl.BlockSpec((tm, Q_B), lambda i, j: (i, 0)),
            pl.BlockSpec((Q_A, tn), lambda i, j: (0, j)),
            pl.BlockSpec((Q_B, tn), lambda i, j: (1, j)),
            pl.BlockSpec((tm, tn), lambda i, j: (i, j)),
        ],
        out_specs=pl.BlockSpec((tm, tn), lambda i, j: (i, j)),
        out_shape=jax.ShapeDtypeStruct((t, d), jnp.float32),
        compiler_params=_params("arbitrary", "arbitrary"),
        name="outproj",
    )(out_a, out_b, w_bf16, w_bf16, x2)


def _swiglu_chunk(h, wg, wu, wd):
    a = _dot(h, wg)
    b = _dot(h, wu)
    act = (a * jax.nn.sigmoid(a) * b).astype(jnp.bfloat16)
    return _dot(act, wd)


def _ffn_kernel(x_hbm, g_ref, wg_ref, wu_ref, wd_ref, o_ref, h_ref, sem, *, tm):
    i = pl.program_id(0)
    k = pl.program_id(1)

    @pl.when(k == 0)
    def _():
        cp = pltpu.make_async_copy(x_hbm.at[pl.ds(pl.multiple_of(i * tm, tm), tm)], o_ref, sem)
        cp.start()
        cp.wait()
        h_ref[...] = _rms(o_ref[...], g_ref[...]).astype(h_ref.dtype)

    bf16 = jnp.bfloat16
    o_ref[...] += _swiglu_chunk(h_ref[...], wg_ref[0].astype(bf16), wu_ref[0].astype(bf16),
                                wd_ref[0].astype(bf16))


def _ffn_dense(x2, gain, wg, wu, wd, layer):
    t, d = x2.shape
    f = wg.shape[-1]
    tm = min(1024, t)
    tf = next(c for c in (512, 256, 128) if f % c == 0)
    return pl.pallas_call(
        functools.partial(_ffn_kernel, tm=tm),
        grid=(t // tm, f // tf),
        in_specs=[
            pl.BlockSpec(memory_space=pl.ANY),
            pl.BlockSpec((1, d), lambda i, k: (0, 0)),
            pl.BlockSpec((1, d, tf), lambda i, k: (layer, 0, k)),
            pl.BlockSpec((1, d, tf), lambda i, k: (layer, 0, k)),
            pl.BlockSpec((1, tf, d), lambda i, k: (layer, k, 0)),
        ],
        out_specs=pl.BlockSpec((tm, d), lambda i, k: (i, 0)),
        out_shape=jax.ShapeDtypeStruct((t, d), jnp.float32),
        scratch_shapes=[pltpu.VMEM((tm, d), jnp.bfloat16), pltpu.SemaphoreType.DMA(())],
        compiler_params=_params("arbitrary", "arbitrary"),
        name="ffn_dense",
    )(x2, gain, wg, wu, wd)


def _router_kernel(x_ref, g_ref, whi_ref, wlo_ref, o_ref, *, n_experts):
    h = _rms(x_ref[...], g_ref[...])
    h_hi = h.astype(jnp.bfloat16)
    h_lo = (h - h_hi.astype(jnp.float32)).astype(jnp.bfloat16)
    logits = _dot(h_hi, whi_ref[...]) + (_dot(h_lo, whi_ref[...]) + _dot(h_hi, wlo_ref[...]))
    lane = lax.broadcasted_iota(jnp.int32, logits.shape, 1)
    logits = jnp.where(lane < n_experts, logits, -jnp.inf)
    v1 = jnp.max(logits, axis=-1, keepdims=True)
    i1 = jnp.min(jnp.where(logits == v1, lane, LANES), axis=-1, keepdims=True)
    rest = jnp.where(lane == i1, -jnp.inf, logits)
    v2 = jnp.max(rest, axis=-1, keepdims=True)
    i2 = jnp.min(jnp.where(rest == v2, lane, LANES), axis=-1, keepdims=True)
    e2 = jnp.exp(v2 - v1)
    g1 = 1.0 / (1.0 + e2)
    g2 = e2 / (1.0 + e2)
    o_ref[...] = jnp.where(lane == 0, i1.astype(jnp.float32),
                           jnp.where(lane == 1, i2.astype(jnp.float32),
                                     jnp.where(lane == 2, g1, jnp.where(lane == 3, g2, 0.0))))


def _router(x2, gain, w_pad, n_experts):
    t, d = x2.shape
    tm = min(512, t)
    w_hi = w_pad.astype(jnp.bfloat16)
    w_lo = (w_pad - w_hi.astype(jnp.float32)).astype(jnp.bfloat16)
    return pl.pallas_call(
        functools.partial(_router_kernel, n_experts=n_experts),
        grid=(t // tm,),
        in_specs=[
            pl.BlockSpec((tm, d), lambda i: (i, 0)),
            pl.BlockSpec((1, d), lambda i: (0, 0)),
            pl.BlockSpec((d, LANES), lambda i: (0, 0)),
            pl.BlockSpec((d, LANES), lambda i: (0, 0)),
        ],
        out_specs=pl.BlockSpec((tm, LANES), lambda i: (i, 0)),
        out_shape=jax.ShapeDtypeStruct((t, LANES), jnp.float32),
        compiler_params=_params("arbitrary"),
        name="router",
    )(x2, gain, w_hi, w_lo)


def _moe_kernel(te_ref, nu_ref, nsub_ref, src_ref, x_hbm, g_ref, wg_ref, wu_ref, wd_ref, o_hbm,
                xg_ref, acc_ref, h_ref, sem_in, sem_out, *, tm, rows_per_step, n_tiles):
    i = pl.program_id(0)
    k = pl.program_id(1)
    nk = pl.num_programs(1)
    n_used = nu_ref[0]
    used = i < n_used
    buf_rows = xg_ref.shape[0]

    def row_copy(tile, r):
        tok = src_ref[tile * tm + jnp.minimum(r, tm - 1)]
        return pltpu.make_async_copy(x_hbm.at[pl.ds(tok, 1)], xg_ref.at[pl.ds(r, 1)], sem_in)

    def wait_rows():
        pltpu.make_async_copy(xg_ref, xg_ref, sem_in).wait()

    def out_copy(tile):
        return pltpu.make_async_copy(
            acc_ref, o_hbm.at[pl.ds(pl.multiple_of(tile * tm, tm), tm)], sem_out)

    @pl.when(jnp.logical_and(i == 0, k == 0))
    def _():
        def start(r, c):
            for u in range(DMA_UNROLL):
                row_copy(0, r * DMA_UNROLL + u).start()
            return c
        lax.fori_loop(0, buf_rows // DMA_UNROLL, start, 0)

    @pl.when(k == 0)
    def _():
        @pl.when(i <= n_used)
        def _():
            wait_rows()

        @pl.when(used)
        def _():
            h_ref[...] = _rms(xg_ref[pl.ds(0, tm), :], g_ref[...]).astype(h_ref.dtype)

        @pl.when(i > 0)
        def _():
            out_copy(i - 1).wait()
        acc_ref[...] = jnp.zeros_like(acc_ref)

    nxt = jnp.minimum(i + 1, n_tiles - 1)
    for j in range(1, tm // MOE_SUB + 1):
        rows = j * MOE_SUB

        @pl.when(jnp.logical_and(used, nsub_ref[i] == j))
        def _(rows=rows):
            for u in range(rows_per_step):
                row_copy(nxt, k * rows_per_step + u).start()
            wg = wg_ref[0, 0].astype(jnp.bfloat16)
            wu = wu_ref[0, 0].astype(jnp.bfloat16)
            wd = wd_ref[0, 0].astype(jnp.bfloat16)
            acc_ref[:rows] += _swiglu_chunk(h_ref[:rows], wg, wu, wd)

    @pl.when(k == nk - 1)
    def _():
        out_copy(i).start()

        @pl.when(i == n_tiles - 1)
        def _():
            out_copy(i).wait()

            @pl.when(used)
            def _():
                wait_rows()


def _moe_ffn(x2, gain, wg, wu, wd, layer, tile_expert, n_used, n_sub, row_src, tm, n_tiles):
    t, d = x2.shape
    f = wg.shape[-1]
    tf = next(c for c in (512, 256, 128) if f % c == 0)
    nk = f // tf
    rows_per_step = -(-tm // nk)
    while (rows_per_step * nk) % DMA_UNROLL:
        rows_per_step += 1
    buf_rows = rows_per_step * nk

    def wmap_col(i, k, te, nu, ns, src):
        return (layer, te[i], 0, jnp.where(i < nu[0], k, nk - 1))

    def wmap_row(i, k, te, nu, ns, src):
        return (layer, te[i], jnp.where(i < nu[0], k, nk - 1), 0)

    return pl.pallas_call(
        functools.partial(_moe_kernel, tm=tm, rows_per_step=rows_per_step, n_tiles=n_tiles),
        grid_spec=pltpu.PrefetchScalarGridSpec(
            num_scalar_prefetch=4,
            grid=(n_tiles, nk),
            in_specs=[
                pl.BlockSpec(memory_space=pl.ANY),
                pl.BlockSpec((1, d), lambda i, k, te, nu, ns, src: (0, 0)),
                pl.BlockSpec((1, 1, d, tf), wmap_col),
                pl.BlockSpec((1, 1, d, tf), wmap_col),
                pl.BlockSpec((1, 1, tf, d), wmap_row),
            ],
            out_specs=pl.BlockSpec(memory_space=pl.ANY),
            scratch_shapes=[
                pltpu.VMEM((buf_rows, d), jnp.float32),
                pltpu.VMEM((tm, d), jnp.float32),
                pltpu.VMEM((tm, d), jnp.bfloat16),
                pltpu.SemaphoreType.DMA(()),
                pltpu.SemaphoreType.DMA(()),
            ],
        ),
        out_shape=jax.ShapeDtypeStruct((n_tiles * tm, d), jnp.float32),
        compiler_params=_params("arbitrary", "arbitrary"),
        name="moe_ffn",
    )(tile_expert, n_used, n_sub, row_src, x2, gain, wg, wu, wd)


def _combine_kernel(pos_ref, x_ref, gate_ref, fg_ref, y_hbm, o_ref, buf_ref, sem, *, tc, t, final):
    i = pl.program_id(0)
    n = pl.num_programs(0)
    slot = i % 2

    def issue(tile, s):
        for r in range(tc):
            for kk in range(TOP_K):
                pltpu.make_async_copy(y_hbm.at[pl.ds(pos_ref[kk * t + tile * tc + r], 1)],
                                      buf_ref.at[s, kk, pl.ds(r, 1)], sem.at[s]).start()

    @pl.when(i == 0)
    def _():
        issue(0, 0)

    for s in range(2):
        @pl.when(jnp.logical_and(i + 1 < n, (i + 1) % 2 == s))
        def _(s=s):
            issue(i + 1, s)

    for kk in range(TOP_K):
        pltpu.make_async_copy(buf_ref.at[slot, kk], buf_ref.at[slot, kk], sem.at[slot]).wait()
    gate = gate_ref[...]
    y = x_ref[...] + gate[:, 2:3] * buf_ref[slot, 0] + gate[:, 3:4] * buf_ref[slot, 1]
    if final:
        y = _rms(y, fg_ref[...])
    o_ref[...] = y


def _combine(x2, routing, pos, y_sorted, final_gain, final):
    t, d = x2.shape
    tc = min(256, t)
    return pl.pallas_call(
        functools.partial(_combine_kernel, tc=tc, t=t, final=final),
        grid_spec=pltpu.PrefetchScalarGridSpec(
            num_scalar_prefetch=1,
            grid=(t // tc,),
            in_specs=[
                pl.BlockSpec((tc, d), lambda i, p: (i, 0)),
                pl.BlockSpec((tc, LANES), lambda i, p: (i, 0)),
                pl.BlockSpec((1, d), lambda i, p: (0, 0)),
                pl.BlockSpec(memory_space=pl.ANY),
            ],
            out_specs=pl.BlockSpec((tc, d), lambda i, p: (i, 0)),
            scratch_shapes=[
                pltpu.VMEM((2, TOP_K, tc, d), jnp.float32),
                pltpu.SemaphoreType.DMA((2,)),
            ],
        ),
        out_shape=jax.ShapeDtypeStruct((t, d), jnp.float32),
        compiler_params=_params("arbitrary"),
        name="combine",
    )(pos, x2, routing, final_gain, y_sorted)


def _final_norm_kernel(x_ref, g_ref, o_ref):
    o_ref[...] = _rms(x_ref[...], g_ref[...])


def _final_norm(x2, gain):
    t, d = x2.shape
    tm = min(1024, t)
    return pl.pallas_call(
        _final_norm_kernel,
        grid=(t // tm,),
        in_specs=[pl.BlockSpec((tm, d), lambda i: (i, 0)), pl.BlockSpec((1, d), lambda i: (0, 0))],
        out_specs=pl.BlockSpec((tm, d), lambda i: (i, 0)),
        out_shape=jax.ShapeDtypeStruct((t, d), jnp.float32),
        compiler_params=_params("arbitrary"),
        name="final_norm",
    )(x2, gain)


def _t5_bucket(rel):
    half = N_BUCKETS // 2
    max_exact = half // 2
    side = jnp.where(rel > 0, half, 0)
    n = jnp.abs(rel)
    nf = jnp.maximum(n, 1).astype(jnp.float32)
    large = max_exact + (jnp.log(nf / max_exact) / math.log(MAX_DISTANCE / max_exact)
                         * (half - max_exact)).astype(jnp.int32)
    large = jnp.minimum(large, half - 1)
    return side + jnp.where(n < max_exact, n, large)


def _window_bias(rel_table):
    qi = jnp.arange(BLOCK, dtype=jnp.int32)[None, :, None]
    si = jnp.arange(3 * BLOCK, dtype=jnp.int32)[None, None, :]
    off = (jnp.arange(3, dtype=jnp.int32) * BLOCK)[:, None, None]
    rel = si - off - qi
    bucket = _t5_bucket(rel)[:, None]
    table = rel_table.astype(jnp.float32).T[None, :, :, None, None]
    bias = jnp.full((3, rel_table.shape[1], BLOCK, 3 * BLOCK), NEG_INF, jnp.float32)
    valid = (jnp.abs(rel) <= WINDOW)[:, None]
    for b in range(N_BUCKETS):
        bias = jnp.where(valid & (bucket == b), table[:, :, b], bias)
    return bias


def _rope_tables(seq):
    pos = jnp.arange(seq, dtype=jnp.int32)
    row = (pos // GRID_W).astype(jnp.float32)
    col = (pos % GRID_W).astype(jnp.float32)
    inv = 1.0 / (ROPE_THETA ** (jnp.arange(0, ROPE_AXIS_DIM, 2, dtype=jnp.float32) / ROPE_AXIS_DIM))
    ar = row[:, None] * inv[None, :]
    ac = col[:, None] * inv[None, :]
    cos = jnp.concatenate([jnp.cos(ar), jnp.cos(ar), jnp.cos(ac), jnp.cos(ac)], axis=-1)
    sin = jnp.concatenate([-jnp.sin(ar), jnp.sin(ar), -jnp.sin(ac), jnp.sin(ac)], axis=-1)
    return cos, sin


def _routing_plan(routing, n_experts, tm, n_tiles):
    t = routing.shape[0]
    experts = jnp.concatenate([routing[:, 0], routing[:, 1]]).astype(jnp.int32)
    onehot = (experts[:, None] == jnp.arange(n_experts, dtype=jnp.int32)[None, :]).astype(jnp.int32)
    csum = jnp.cumsum(onehot, axis=0)
    rank = jnp.sum(onehot * csum, axis=1) - 1
    counts = csum[-1]
    tiles_per = (counts + tm - 1) // tm
    tile_end = jnp.cumsum(tiles_per)
    tile_start = tile_end - tiles_per
    pos = (jnp.sum(onehot * (tile_start * tm)[None, :], axis=1) + rank).astype(jnp.int32)
    tok = jnp.arange(TOP_K * t, dtype=jnp.int32) % t
    row_src = jnp.zeros((n_tiles * tm,), jnp.int32).at[pos].set(tok)
    n_used = tile_end[-1].astype(jnp.int32)
    tile_id = jnp.minimum(jnp.arange(n_tiles, dtype=jnp.int32), n_used - 1)
    tile_expert = jnp.sum((tile_id[:, None] >= tile_end[None, :]).astype(jnp.int32), axis=1)
    filled = jnp.clip(counts[tile_expert] - (tile_id - tile_start[tile_expert]) * tm, 1, tm)
    n_sub = (filled + MOE_SUB - 1) // MOE_SUB
    return (tile_expert.astype(jnp.int32), n_used.reshape(1), n_sub.astype(jnp.int32),
            row_src, pos)


def kernel(x, norm_mix, w_in, w_out, sink_a, rel_bias, q_norm_b, k_norm_b, norm_ffn,
           w_dense_gate, w_dense_up, w_dense_down, w_router, w_exp_gate, w_exp_up,
           w_exp_down, norm_final):
    batch, seq, d = x.shape
    depth = w_in.shape[0]
    n_experts = w_router.shape[-1]
    t = batch * seq
    bf16 = jnp.bfloat16
    x2 = x.reshape(t, d)
    cos, sin = _rope_tables(seq)
    bias = _window_bias(rel_bias)
    moe_tm = min(1024, t)
    moe_tiles = (TOP_K * t) // moe_tm + n_experts
    for l in range(depth):
        proj = _proj(x2, norm_mix[l][None], w_in[l], cos, sin,
                     q_norm_b[l][None], k_norm_b[l][None], seq)
        out_a = _window_attention(proj, sink_a[l], bias, batch, seq)
        out_b = _global_attention(proj, batch, seq)
        x2 = _outproj(out_a, out_b, w_out[l].astype(bf16), x2)
        last = l == depth - 1
        i = l // 2
        if l % 2 == 0:
            x2 = _ffn_dense(x2, norm_ffn[l][None], w_dense_gate, w_dense_up, w_dense_down, i)
            if last:
                x2 = _final_norm(x2, norm_final[None])
        else:
            w_pad = jnp.pad(w_router[i], ((0, 0), (0, LANES - n_experts)))
            routing = _router(x2, norm_ffn[l][None], w_pad, n_experts)
            tile_expert, n_used, n_sub, row_src, pos = _routing_plan(
                routing, n_experts, moe_tm, moe_tiles)
            y_sorted = _moe_ffn(x2, norm_ffn[l][None], w_exp_gate, w_exp_up, w_exp_down, i,
                                tile_expert, n_used, n_sub, row_src, moe_tm, moe_tiles)
            x2 = _combine(x2, routing, pos, y_sorted, norm_final[None], last)
    return x2.reshape(batch, seq, d)
```

```python
import functools
import math

import jax
import jax.numpy as jnp
from jax import lax
from jax.experimental import pallas as pl
from jax.experimental.pallas import tpu as pltpu

HEAD_DIM = 128
N_HEADS_A = 8
N_KV_A = 2
N_HEADS_B = 8
N_KV_B = 2
GROUP_A = N_HEADS_A // N_KV_A
GROUP_B = N_HEADS_B // N_KV_B
Q_A = N_HEADS_A * HEAD_DIM
KV_A = N_KV_A * HEAD_DIM
Q_B = N_HEADS_B * HEAD_DIM
KV_B = N_KV_B * HEAD_DIM
HALF_IN = Q_A + 2 * KV_A
WINDOW = 128
BLOCK = 128
GRID_W = 64
ROPE_THETA = 10000.0
ROPE_AXIS_DIM = HEAD_DIM // 2
N_BUCKETS = 32
MAX_DISTANCE = 128
TOP_K = 2
EPS = 1e-6
NEG_INF = -1e30
LANES = 128
QK_SCALE = HEAD_DIM ** -0.5
LOG2_E = math.log2(math.e)
VMEM_LIMIT = 56 * 1024 * 1024
MOE_SUB = 256
DMA_UNROLL = 8


def _params(*sem):
    return pltpu.CompilerParams(dimension_semantics=sem, vmem_limit_bytes=VMEM_LIMIT)


def _rms(x, g):
    return x * lax.rsqrt(jnp.mean(x * x, axis=-1, keepdims=True) + EPS) * g


def _dot(a, b):
    return jnp.dot(a, b, preferred_element_type=jnp.float32)


def _dot_t(a, b):
    return lax.dot_general(a, b, (((1,), (1,)), ((), ())), preferred_element_type=jnp.float32)


def _rope(x, cos, sin_signed):
    lane = lax.broadcasted_iota(jnp.int32, x.shape, 1)
    swapped = jnp.where((lane & 32) == 0,
                        pltpu.roll(x, LANES - 32, axis=1), pltpu.roll(x, 32, axis=1))
    return x * cos + swapped * sin_signed


def _proj_kernel(x_ref, g_ref, w_ref, cos_ref, sin_ref, qg_ref, kg_ref, o_ref, wbf_ref, *,
                 q_scale_b):
    @pl.when(pl.program_id(1) == 0)
    def _():
        wbf_ref[...] = w_ref[...].astype(wbf_ref.dtype)

    h = _rms(x_ref[...], g_ref[...]).astype(jnp.bfloat16)
    group = pl.program_id(0)
    pw = 2 * HEAD_DIM
    n_pieces = HALF_IN // pw

    def piece(c):
        return _dot(h, wbf_ref[:, c * pw:(c + 1) * pw])

    def pipelined(epilogue):
        y = piece(0)
        for c in range(n_pieces):
            nxt = piece(c + 1) if c + 1 < n_pieces else None
            o_ref[:, c * pw:(c + 1) * pw] = epilogue(c, y).astype(o_ref.dtype)
            y = nxt

    @pl.when(group == 0)
    def _():
        pipelined(lambda c, y: y * QK_SCALE if c * pw < Q_A else y)

    @pl.when(group == 1)
    def _():
        cos = cos_ref[...]
        sin = sin_ref[...]

        def epilogue(c, y):
            if c * pw >= Q_B + KV_B:
                return y
            is_q = c * pw < Q_B
            gain = qg_ref[...] if is_q else kg_ref[...]
            halves = [_rms(y[:, j * HEAD_DIM:(j + 1) * HEAD_DIM], gain) for j in range(2)]
            halves = [_rope(v, cos, sin) for v in halves]
            if is_q:
                halves = [v * q_scale_b for v in halves]
            return jnp.concatenate(halves, axis=1)

        pipelined(epilogue)


def _proj(x2, gain, w, cos, sin, qg, kg, seq):
    t, d = x2.shape
    tm = min(512, seq)
    spt = seq // tm
    return pl.pallas_call(
        functools.partial(_proj_kernel, q_scale_b=QK_SCALE * LOG2_E),
        grid=(2, t // tm),
        in_specs=[
            pl.BlockSpec((tm, d), lambda j, i: (i, 0)),
            pl.BlockSpec((1, d), lambda j, i: (0, 0)),
            pl.BlockSpec((d, HALF_IN), lambda j, i: (0, j)),
            pl.BlockSpec((tm, HEAD_DIM), lambda j, i: (i % spt, 0)),
            pl.BlockSpec((tm, HEAD_DIM), lambda j, i: (i % spt, 0)),
            pl.BlockSpec((1, HEAD_DIM), lambda j, i: (0, 0)),
            pl.BlockSpec((1, HEAD_DIM), lambda j, i: (0, 0)),
        ],
        out_specs=pl.BlockSpec((tm, HALF_IN), lambda j, i: (i, j)),
        out_shape=jax.ShapeDtypeStruct((t, 2 * HALF_IN), jnp.bfloat16),
        scratch_shapes=[pltpu.VMEM((d, HALF_IN), jnp.bfloat16)],
        compiler_params=_params("arbitrary", "arbitrary"),
        name="proj",
    )(x2, gain, w, cos, sin, qg, kg)


def _win_kernel(sink_ref, q_ref, k_ref, v_ref, bias_ref, o_ref, *, qb, nb, seq):
    step = pl.program_id(1)
    for jb in range(qb):
        n = step * qb + jb
        ws = pl.multiple_of(jnp.clip((n - 1) * BLOCK, 0, seq - 3 * BLOCK), BLOCK)
        variant = jnp.where(n == 0, 0, jnp.where(n == nb - 1, 2, 1))
        rows = slice(jb * BLOCK, (jb + 1) * BLOCK)
        heads = range(N_HEADS_A)
        hcols = [slice(hd * HEAD_DIM, (hd + 1) * HEAD_DIM) for hd in heads]
        kws = [k_ref[pl.ds(ws, 3 * BLOCK), kvh * HEAD_DIM:(kvh + 1) * HEAD_DIM] for kvh in range(N_KV_A)]
        vws = [v_ref[pl.ds(ws, 3 * BLOCK), kvh * HEAD_DIM:(kvh + 1) * HEAD_DIM] for kvh in range(N_KV_A)]
        ss = [_dot_t(q_ref[rows, hcols[hd]], kws[hd // GROUP_A]) + bias_ref[variant, hd] for hd in heads]
        ms = [jnp.maximum(jnp.max(ss[hd], axis=-1, keepdims=True), sink_ref[hd]) for hd in heads]
        ps = [jnp.exp(ss[hd] - ms[hd]) for hd in heads]
        ds = [jnp.sum(ps[hd], axis=-1, keepdims=True) + jnp.exp(sink_ref[hd] - ms[hd]) for hd in heads]
        os_ = [_dot(ps[hd].astype(jnp.bfloat16), vws[hd // GROUP_A]) for hd in heads]
        for hd in heads:
            o_ref[rows, hcols[hd]] = (os_[hd] / ds[hd]).astype(o_ref.dtype)


def _window_attention(proj, sink, bias, batch, seq, qb=4):
    nb = seq // BLOCK
    qb = min(qb, nb)
    spb = seq // (qb * BLOCK)
    kern = functools.partial(_win_kernel, qb=qb, nb=nb, seq=seq)
    return pl.pallas_call(
        kern,
        grid_spec=pltpu.PrefetchScalarGridSpec(
            num_scalar_prefetch=1,
            grid=(batch, spb),
            in_specs=[
                pl.BlockSpec((qb * BLOCK, Q_A), lambda b, s, sk: (b * spb + s, 0)),
                pl.BlockSpec((seq, KV_A), lambda b, s, sk: (b, Q_A // KV_A)),
                pl.BlockSpec((seq, KV_A), lambda b, s, sk: (b, Q_A // KV_A + 1)),
                pl.BlockSpec(bias.shape, lambda b, s, sk: (0, 0, 0, 0)),
            ],
            out_specs=pl.BlockSpec((qb * BLOCK, Q_A), lambda b, s, sk: (b * spb + s, 0)),
        ),
        out_shape=jax.ShapeDtypeStruct((batch * seq, Q_A), jnp.bfloat16),
        compiler_params=_params("arbitrary", "arbitrary"),
        name="window_attention",
    )(sink, proj, proj, proj, bias)


def _glob_kernel(q0_ref, q1_ref, k_ref, v_ref, *rest):
    if len(rest) == 1:
        (o_ref,) = rest
    else:
        w_in_refs, o_ref, w_out_refs = rest[:3], rest[3], rest[4:]
        for w_ref, wo_ref in zip(w_in_refs, w_out_refs):
            wo_ref[...] = w_ref[0].astype(wo_ref.dtype)
    q_refs = (q0_ref, q1_ref)

    def scores(hd):
        kv, g = divmod(hd, GROUP_B)
        return _dot_t(q_refs[kv][:, g * HEAD_DIM:(g + 1) * HEAD_DIM],
                      k_ref[:, kv * HEAD_DIM:(kv + 1) * HEAD_DIM])

    s_next = scores(0)
    for hd in range(N_HEADS_B):
        s = s_next
        if hd + 1 < N_HEADS_B:
            s_next = scores(hd + 1)
        kv = hd // GROUP_B
        m = jnp.max(s, axis=-1, keepdims=True)
        p = jnp.exp2(s - m)
        denom = jnp.sum(p, axis=-1, keepdims=True)
        o = _dot(p.astype(jnp.bfloat16), v_ref[:, kv * HEAD_DIM:(kv + 1) * HEAD_DIM]) / denom
        o_ref[:, hd * HEAD_DIM:(hd + 1) * HEAD_DIM] = o.astype(o_ref.dtype)


def _global_attention(proj, batch, seq, dense_weights=None, layer=0):
    tq = min(256, seq)
    spb = seq // tq
    n_steps = batch * spb
    gw = GROUP_B * HEAD_DIM
    in_specs = [
        pl.BlockSpec((tq, gw), lambda b, s: (b * spb + s, HALF_IN // gw)),
        pl.BlockSpec((tq, gw), lambda b, s: (b * spb + s, HALF_IN // gw + 1)),
        pl.BlockSpec((seq, KV_B), lambda b, s: (b, (HALF_IN + Q_B) // KV_B)),
        pl.BlockSpec((seq, KV_B), lambda b, s: (b, (HALF_IN + Q_B) // KV_B + 1)),
    ]
    out_specs = [pl.BlockSpec((tq, Q_B), lambda b, s: (b * spb + s, 0))]
    out_shape = [jax.ShapeDtypeStruct((batch * seq, Q_B), jnp.bfloat16)]
    args = [proj, proj, proj, proj]
    if dense_weights is not None:
        for w in dense_weights:
            rows, cols = w.shape[1] // n_steps, w.shape[2]
            in_specs.append(pl.BlockSpec((1, rows, cols), lambda b, s: (layer, b * spb + s, 0)))
            out_specs.append(pl.BlockSpec((rows, cols), lambda b, s: (b * spb + s, 0)))
            out_shape.append(jax.ShapeDtypeStruct(w.shape[1:], jnp.bfloat16))
            args.append(w)
    outs = pl.pallas_call(
        _glob_kernel,
        grid=(batch, spb),
        in_specs=in_specs,
        out_specs=out_specs,
        out_shape=out_shape,
        compiler_params=_params("arbitrary", "arbitrary"),
        name="global_attention",
    )(*args)
    return outs[0], tuple(outs[1:])


def _can_convert_in_attention(batch, seq, weights):
    n_steps = batch * (seq // min(256, seq))
    return all(w.shape[1] % (16 * n_steps) == 0 for w in weights)


def _outproj_kernel(a_ref, b_ref, wa_ref, wb_ref, x_ref, o_ref):
    o_ref[...] = x_ref[...] + _dot(a_ref[...], wa_ref[...]) + _dot(b_ref[...], wb_ref[...])


def _outproj(out_a, out_b, w_bf16, x2):
    t, d = x2.shape
    tm = min(512, t)
    tn = min(2048, d)
    assert Q_A == Q_B
    return pl.pallas_call(
        _outproj_kernel,
        grid=(t // tm, d // tn),
        in_specs=[
            pl.BlockSpec((tm, Q_A), lambda i, j: (i, 0)),
            pl.BlockSpec((tm, Q_B), lambda i, j: (i, 0)),
            pl.BlockSpec((Q_A, tn), lambda i, j: (0, j)),
            pl.BlockSpec((Q_B, tn), lambda i, j: (1, j)),
            pl.BlockSpec((tm, tn), lambda i, j: (i, j)),
        ],
        out_specs=pl.BlockSpec((tm, tn), lambda i, j: (i, j)),
        out_shape=jax.ShapeDtypeStruct((t, d), jnp.float32),
        compiler_params=_params("arbitrary", "arbitrary"),
        name="outproj",
    )(out_a, out_b, w_bf16, w_bf16, x2)


def _swiglu_chunk(h, wg, wu, wd):
    a = _dot(h, wg)
    b = _dot(h, wu)
    act = (a * jax.nn.sigmoid(a) * b).astype(jnp.bfloat16)
    return _dot(act, wd)


def _ffn_kernel(x_ref, g_ref, wg_ref, wu_ref, wd_ref, o_ref, h_ref):
    k = pl.program_id(1)

    @pl.when(k == 0)
    def _():
        x = x_ref[...]
        h_ref[...] = _rms(x, g_ref[...]).astype(h_ref.dtype)
        o_ref[...] = x

    o_ref[...] += _swiglu_chunk(h_ref[...], wg_ref[...], wu_ref[...], wd_ref[...])


def _ffn_dense(x2, gain, wg, wu, wd):
    t, d = x2.shape
    f = wg.shape[-1]
    tm = min(1024, t)
    tf = next(c for c in (512, 256, 128) if f % c == 0)
    return pl.pallas_call(
        _ffn_kernel,
        grid=(t // tm, f // tf),
        in_specs=[
            pl.BlockSpec((tm, d), lambda i, k: (i, 0)),
            pl.BlockSpec((1, d), lambda i, k: (0, 0)),
            pl.BlockSpec((d, tf), lambda i, k: (0, k)),
            pl.BlockSpec((d, tf), lambda i, k: (0, k)),
            pl.BlockSpec((tf, d), lambda i, k: (k, 0)),
        ],
        out_specs=pl.BlockSpec((tm, d), lambda i, k: (i, 0)),
        out_shape=jax.ShapeDtypeStruct((t, d), jnp.float32),
        scratch_shapes=[pltpu.VMEM((tm, d), jnp.bfloat16)],
        compiler_params=_params("arbitrary", "arbitrary"),
        name="ffn_dense",
    )(x2, gain, wg, wu, wd)


def _router_kernel(x_ref, g_ref, whi_ref, wlo_ref, o_ref, *, n_experts):
    h = _rms(x_ref[...], g_ref[...])
    h_hi = h.astype(jnp.bfloat16)
    h_lo = (h - h_hi.astype(jnp.float32)).astype(jnp.bfloat16)
    logits = _dot(h_hi, whi_ref[...]) + (_dot(h_lo, whi_ref[...]) + _dot(h_hi, wlo_ref[...]))
    lane = lax.broadcasted_iota(jnp.int32, logits.shape, 1)
    logits = jnp.where(lane < n_experts, logits, -jnp.inf)
    v1 = jnp.max(logits, axis=-1, keepdims=True)
    i1 = jnp.min(jnp.where(logits == v1, lane, LANES), axis=-1, keepdims=True)
    rest = jnp.where(lane == i1, -jnp.inf, logits)
    v2 = jnp.max(rest, axis=-1, keepdims=True)
    i2 = jnp.min(jnp.where(rest == v2, lane, LANES), axis=-1, keepdims=True)
    e2 = jnp.exp(v2 - v1)
    g1 = 1.0 / (1.0 + e2)
    g2 = e2 / (1.0 + e2)
    o_ref[...] = jnp.where(lane == 0, i1.astype(jnp.float32),
                           jnp.where(lane == 1, i2.astype(jnp.float32),
                                     jnp.where(lane == 2, g1, jnp.where(lane == 3, g2, 0.0))))


def _router(x2, gain, w_pad, n_experts):
    t, d = x2.shape
    tm = min(512, t)
    w_hi = w_pad.astype(jnp.bfloat16)
    w_lo = (w_pad - w_hi.astype(jnp.float32)).astype(jnp.bfloat16)
    return pl.pallas_call(
        functools.partial(_router_kernel, n_experts=n_experts),
        grid=(t // tm,),
        in_specs=[
            pl.BlockSpec((tm, d), lambda i: (i, 0)),
            pl.BlockSpec((1, d), lambda i: (0, 0)),
            pl.BlockSpec((d, LANES), lambda i: (0, 0)),
            pl.BlockSpec((d, LANES), lambda i: (0, 0)),
        ],
        out_specs=pl.BlockSpec((tm, LANES), lambda i: (i, 0)),
        out_shape=jax.ShapeDtypeStruct((t, LANES), jnp.float32),
        compiler_params=_params("arbitrary"),
        name="router",
    )(x2, gain, w_hi, w_lo)


def _moe_kernel(te_ref, nu_ref, nsub_ref, src_ref, x_hbm, g_ref, wg_ref, wu_ref, wd_ref, o_hbm,
                xg_ref, acc_ref, h_ref, sem_in, sem_out, *, tm, rows_per_step, n_tiles):
    i = pl.program_id(0)
    k = pl.program_id(1)
    nk = pl.num_programs(1)
    n_used = nu_ref[0]
    used = i < n_used
    buf_rows = xg_ref.shape[0]

    def row_copy(tile, r):
        tok = src_ref[tile * tm + jnp.minimum(r, tm - 1)]
        return pltpu.make_async_copy(x_hbm.at[pl.ds(tok, 1)], xg_ref.at[pl.ds(r, 1)], sem_in)

    def wait_rows():
        pltpu.make_async_copy(xg_ref, xg_ref, sem_in).wait()

    def out_copy(tile):
        return pltpu.make_async_copy(
            acc_ref, o_hbm.at[pl.ds(pl.multiple_of(tile * tm, tm), tm)], sem_out)

    @pl.when(jnp.logical_and(i == 0, k == 0))
    def _():
        def start(r, c):
            for u in range(DMA_UNROLL):
                row_copy(0, r * DMA_UNROLL + u).start()
            return c
        lax.fori_loop(0, buf_rows // DMA_UNROLL, start, 0)

    @pl.when(k == 0)
    def _():
        @pl.when(i <= n_used)
        def _():
            wait_rows()

        @pl.when(used)
        def _():
            h_ref[...] = _rms(xg_ref[pl.ds(0, tm), :], g_ref[...]).astype(h_ref.dtype)

        @pl.when(i > 0)
        def _():
            out_copy(i - 1).wait()
        acc_ref[...] = jnp.zeros_like(acc_ref)

    nxt = jnp.minimum(i + 1, n_tiles - 1)
    for j in range(1, tm // MOE_SUB + 1):
        rows = j * MOE_SUB

        @pl.when(jnp.logical_and(used, nsub_ref[i] == j))
        def _(rows=rows):
            for u in range(rows_per_step):
                row_copy(nxt, k * rows_per_step + u).start()
            wg = wg_ref[0, 0].astype(jnp.bfloat16)
            wu = wu_ref[0, 0].astype(jnp.bfloat16)
            wd = wd_ref[0, 0].astype(jnp.bfloat16)
            acc_ref[:rows] += _swiglu_chunk(h_ref[:rows], wg, wu, wd)

    @pl.when(k == nk - 1)
    def _():
        out_copy(i).start()

        @pl.when(i == n_tiles - 1)
        def _():
            out_copy(i).wait()

            @pl.when(used)
            def _():
                wait_rows()


def _moe_ffn(x2, gain, wg, wu, wd, layer, tile_expert, n_used, n_sub, row_src, tm, n_tiles):
    t, d = x2.shape
    f = wg.shape[-1]
    tf = next(c for c in ((512, 256, 128) if tm <= 1024 else (256, 128)) if f % c == 0)
    nk = f // tf
    rows_per_step = -(-tm // nk)
    while (rows_per_step * nk) % DMA_UNROLL:
        rows_per_step += 1
    buf_rows = rows_per_step * nk

    def wmap_col(i, k, te, nu, ns, src):
        return (layer, te[i], 0, jnp.where(i < nu[0], k, nk - 1))

    def wmap_row(i, k, te, nu, ns, src):
        return (layer, te[i], jnp.where(i < nu[0], k, nk - 1), 0)

    return pl.pallas_call(
        functools.partial(_moe_kernel, tm=tm, rows_per_step=rows_per_step, n_tiles=n_tiles),
        grid_spec=pltpu.PrefetchScalarGridSpec(
            num_scalar_prefetch=4,
            grid=(n_tiles, nk),
            in_specs=[
                pl.BlockSpec(memory_space=pl.ANY),
                pl.BlockSpec((1, d), lambda i, k, te, nu, ns, src: (0, 0)),
                pl.BlockSpec((1, 1, d, tf), wmap_col),
                pl.BlockSpec((1, 1, d, tf), wmap_col),
                pl.BlockSpec((1, 1, tf, d), wmap_row),
            ],
            out_specs=pl.BlockSpec(memory_space=pl.ANY),
            scratch_shapes=[
                pltpu.VMEM((buf_rows, d), jnp.float32),
                pltpu.VMEM((tm, d), jnp.float32),
                pltpu.VMEM((tm, d), jnp.bfloat16),
                pltpu.SemaphoreType.DMA(()),
                pltpu.SemaphoreType.DMA(()),
            ],
        ),
        out_shape=jax.ShapeDtypeStruct((n_tiles * tm, d), jnp.float32),
        compiler_params=_params("arbitrary", "arbitrary"),
        name="moe_ffn",
    )(tile_expert, n_used, n_sub, row_src, x2, gain, wg, wu, wd)


def _combine_kernel(pos_ref, x_ref, gate_ref, fg_ref, y_hbm, o_ref, buf_ref, sem, *, tc, t, final):
    i = pl.program_id(0)
    n = pl.num_programs(0)
    slot = i % 2

    def issue(tile, s):
        for r in range(tc):
            for kk in range(TOP_K):
                pltpu.make_async_copy(y_hbm.at[pl.ds(pos_ref[kk * t + tile * tc + r], 1)],
                                      buf_ref.at[s, kk, pl.ds(r, 1)], sem.at[s]).start()

    @pl.when(i == 0)
    def _():
        issue(0, 0)

    for s in range(2):
        @pl.when(jnp.logical_and(i + 1 < n, (i + 1) % 2 == s))
        def _(s=s):
            issue(i + 1, s)

    for kk in range(TOP_K):
        pltpu.make_async_copy(buf_ref.at[slot, kk], buf_ref.at[slot, kk], sem.at[slot]).wait()
    gate = gate_ref[...]
    y = x_ref[...] + gate[:, 2:3] * buf_ref[slot, 0] + gate[:, 3:4] * buf_ref[slot, 1]
    if final:
        y = _rms(y, fg_ref[...])
    o_ref[...] = y


def _combine(x2, routing, pos, y_sorted, final_gain, final):
    t, d = x2.shape
    tc = min(256, t)
    return pl.pallas_call(
        functools.partial(_combine_kernel, tc=tc, t=t, final=final),
        grid_spec=pltpu.PrefetchScalarGridSpec(
            num_scalar_prefetch=1,
            grid=(t // tc,),
            in_specs=[
                pl.BlockSpec((tc, d), lambda i, p: (i, 0)),
                pl.BlockSpec((tc, LANES), lambda i, p: (i, 0)),
                pl.BlockSpec((1, d), lambda i, p: (0, 0)),
                pl.BlockSpec(memory_space=pl.ANY),
            ],
            out_specs=pl.BlockSpec((tc, d), lambda i, p: (i, 0)),
            scratch_shapes=[
                pltpu.VMEM((2, TOP_K, tc, d), jnp.float32),
                pltpu.SemaphoreType.DMA((2,)),
            ],
        ),
        out_shape=jax.ShapeDtypeStruct((t, d), jnp.float32),
        compiler_params=_params("arbitrary"),
        name="combine",
    )(pos, x2, routing, final_gain, y_sorted)


def _final_norm_kernel(x_ref, g_ref, o_ref):
    o_ref[...] = _rms(x_ref[...], g_ref[...])


def _final_norm(x2, gain):
    t, d = x2.shape
    tm = min(1024, t)
    return pl.pallas_call(
        _final_norm_kernel,
        grid=(t // tm,),
        in_specs=[pl.BlockSpec((tm, d), lambda i: (i, 0)), pl.BlockSpec((1, d), lambda i: (0, 0))],
        out_specs=pl.BlockSpec((tm, d), lambda i: (i, 0)),
        out_shape=jax.ShapeDtypeStruct((t, d), jnp.float32),
        compiler_params=_params("arbitrary"),
        name="final_norm",
    )(x2, gain)


def _t5_bucket(rel):
    half = N_BUCKETS // 2
    max_exact = half // 2
    side = jnp.where(rel > 0, half, 0)
    n = jnp.abs(rel)
    nf = jnp.maximum(n, 1).astype(jnp.float32)
    large = max_exact + (jnp.log(nf / max_exact) / math.log(MAX_DISTANCE / max_exact)
                         * (half - max_exact)).astype(jnp.int32)
    large = jnp.minimum(large, half - 1)
    return side + jnp.where(n < max_exact, n, large)


def _window_bias(rel_table):
    qi = jnp.arange(BLOCK, dtype=jnp.int32)[None, :, None]
    si = jnp.arange(3 * BLOCK, dtype=jnp.int32)[None, None, :]
    off = (jnp.arange(3, dtype=jnp.int32) * BLOCK)[:, None, None]
    rel = si - off - qi
    bucket = _t5_bucket(rel)[:, None]
    table = rel_table.astype(jnp.float32).T[None, :, :, None, None]
    bias = jnp.full((3, rel_table.shape[1], BLOCK, 3 * BLOCK), NEG_INF, jnp.float32)
    valid = (jnp.abs(rel) <= WINDOW)[:, None]
    for b in range(N_BUCKETS):
        bias = jnp.where(valid & (bucket == b), table[:, :, b], bias)
    return bias


def _rope_tables(seq):
    pos = jnp.arange(seq, dtype=jnp.int32)
    row = (pos // GRID_W).astype(jnp.float32)
    col = (pos % GRID_W).astype(jnp.float32)
    inv = 1.0 / (ROPE_THETA ** (jnp.arange(0, ROPE_AXIS_DIM, 2, dtype=jnp.float32) / ROPE_AXIS_DIM))
    ar = row[:, None] * inv[None, :]
    ac = col[:, None] * inv[None, :]
    cos = jnp.concatenate([jnp.cos(ar), jnp.cos(ar), jnp.cos(ac), jnp.cos(ac)], axis=-1)
    sin = jnp.concatenate([-jnp.sin(ar), jnp.sin(ar), -jnp.sin(ac), jnp.sin(ac)], axis=-1)
    return cos, sin


def _routing_plan(routing, n_experts, tm, n_tiles):
    t = routing.shape[0]
    experts = jnp.concatenate([routing[:, 0], routing[:, 1]]).astype(jnp.int32)
    onehot = (experts[:, None] == jnp.arange(n_experts, dtype=jnp.int32)[None, :]).astype(jnp.int32)
    csum = jnp.cumsum(onehot, axis=0)
    rank = jnp.sum(onehot * csum, axis=1) - 1
    counts = csum[-1]
    tiles_per = (counts + tm - 1) // tm
    tile_end = jnp.cumsum(tiles_per)
    tile_start = tile_end - tiles_per
    pos = (jnp.sum(onehot * (tile_start * tm)[None, :], axis=1) + rank).astype(jnp.int32)
    tok = jnp.arange(TOP_K * t, dtype=jnp.int32) % t
    row_src = jnp.zeros((n_tiles * tm,), jnp.int32).at[pos].set(tok)
    n_used = tile_end[-1].astype(jnp.int32)
    tile_id = jnp.minimum(jnp.arange(n_tiles, dtype=jnp.int32), n_used - 1)
    tile_expert = jnp.sum((tile_id[:, None] >= tile_end[None, :]).astype(jnp.int32), axis=1)
    filled = jnp.clip(counts[tile_expert] - (tile_id - tile_start[tile_expert]) * tm, 1, tm)
    n_sub = (filled + MOE_SUB - 1) // MOE_SUB
    return (tile_expert.astype(jnp.int32), n_used.reshape(1), n_sub.astype(jnp.int32),
            row_src, pos)


def kernel(x, norm_mix, w_in, w_out, sink_a, rel_bias, q_norm_b, k_norm_b, norm_ffn,
           w_dense_gate, w_dense_up, w_dense_down, w_router, w_exp_gate, w_exp_up,
           w_exp_down, norm_final):
    batch, seq, d = x.shape
    depth = w_in.shape[0]
    n_experts = w_router.shape[-1]
    t = batch * seq
    bf16 = jnp.bfloat16
    x2 = x.reshape(t, d)
    cos, sin = _rope_tables(seq)
    bias = _window_bias(rel_bias)
    moe_tm = min(1536, t)
    moe_tiles = (TOP_K * t) // moe_tm + n_experts
    for l in range(depth):
        proj = _proj(x2, norm_mix[l][None], w_in[l], cos, sin,
                     q_norm_b[l][None], k_norm_b[l][None], seq)
        out_a = _window_attention(proj, sink_a[l], bias, batch, seq)
        last = l == depth - 1
        i = l // 2
        if l % 2 == 0:
            dense_w = (w_dense_gate, w_dense_up, w_dense_down)
            if _can_convert_in_attention(batch, seq, dense_w):
                out_b, dense_bf16 = _global_attention(proj, batch, seq, dense_w, i)
            else:
                out_b, _ = _global_attention(proj, batch, seq)
                dense_bf16 = tuple(w[i].astype(bf16) for w in dense_w)
        else:
            out_b, _ = _global_attention(proj, batch, seq)
        x2 = _outproj(out_a, out_b, w_out[l].astype(bf16), x2)
        if l % 2 == 0:
            x2 = _ffn_dense(x2, norm_ffn[l][None], *dense_bf16)
            if last:
                x2 = _final_norm(x2, norm_final[None])
        else:
            w_pad = jnp.pad(w_router[i], ((0, 0), (0, LANES - n_experts)))
            routing = _router(x2, norm_ffn[l][None], w_pad, n_experts)
            tile_expert, n_used, n_sub, row_src, pos = _routing_plan(
                routing, n_experts, moe_tm, moe_tiles)
            y_sorted = _moe_ffn(x2, norm_ffn[l][None], w_exp_gate, w_exp_up, w_exp_down, i,
                                tile_expert, n_used, n_sub, row_src, moe_tm, moe_tiles)
            x2 = _combine(x2, routing, pos, y_sorted, norm_final[None], last)
    return x2.reshape(batch, seq, d)
```
